```python
import jax, jax.numpy as jnp
from jax import lax
import numpy as np

D_MODEL = 1024
BATCH = 2
SEQ = 8192
DEPTH = 4
DEC_BATCH = 128
DEC_SEQ = 1
PAST_LEN = 2048
PAGE_SIZE = 128

N_A_LAYERS = DEPTH // 2
N_B_LAYERS = DEPTH - N_A_LAYERS
N_HEADS = 16
HEAD_DIM = D_MODEL // N_HEADS
N_KV_HEADS = 4
GROUP = N_HEADS // N_KV_HEADS
CMP_LEN = 32
CMP_STRIDE = 16
CMP_HID = 4 * HEAD_DIM
SLC_BLOCK = 64
TOP_N = 16
WINDOW = 512
Q_BLOCK = 128
CONV_W = 3
D_FF = ((8 * D_MODEL // 3 + 127) // 128) * 128
MACARON_W = 0.5
ROPE_THETA = 10000.0
EPS = 1e-6
FORCE_BONUS = 1e3
NEG = -1e30
N_MOD = 9

kernel_name = "yoco_shortconv_nsa_macaron_step"


def rms_norm(x, g):
    x32 = x.astype(jnp.float32)
    y = x32 * lax.rsqrt(jnp.mean(x32 * x32, -1, keepdims=True) + EPS)
    return (y * g.astype(jnp.float32)).astype(x.dtype)


def masked_softmax(s, mask):
    s = jnp.where(mask, s.astype(jnp.float32), NEG)
    e = jnp.exp(s - jnp.max(s, -1, keepdims=True)) * mask
    return e / jnp.maximum(jnp.sum(e, -1, keepdims=True), 1e-30)


def rope(x, pos):
    half = HEAD_DIM // 2
    inv = ROPE_THETA ** (-jnp.arange(half, dtype=jnp.float32) / half)
    ang = pos.astype(jnp.float32)[:, None] * inv[None]
    cos, sin = jnp.cos(ang)[:, None, :], jnp.sin(ang)[:, None, :]
    x32 = x.astype(jnp.float32)
    x1, x2 = x32[..., :half], x32[..., half:]
    return jnp.concatenate([x1 * cos - x2 * sin, x2 * cos + x1 * sin], -1).astype(x.dtype)


def pre(x, g, mod, s):
    return rms_norm(x, g) * (1 + mod[:, :, 3 * s + 1]) + mod[:, :, 3 * s]


def post(x, o, g, mod, s, w):
    return x + w * mod[:, :, 3 * s + 2] * rms_norm(o, g)


def swiglu(h, w_in, w_out):
    a, b = jnp.split(h @ w_in, 2, axis=-1)
    return (jax.nn.silu(a) * b) @ w_out


def conv_mixer(h, hist, w_in, conv_k, w_out):
    bg, cg, v = jnp.split(h @ w_in, 3, axis=-1)
    u = cg * v
    up = jnp.concatenate([hist.astype(u.dtype), u], axis=1)
    L = u.shape[1]
    y = sum(conv_k[j] * up[:, j:j + L] for j in range(CONV_W))
    return (bg * y) @ w_out, up[:, up.shape[1] - (CONV_W - 1):]


def shared_kv_rows(x, pos, kv_norm_g, w_kv):
    B, L, _ = x.shape
    kv = (rms_norm(x, kv_norm_g) @ w_kv).reshape(B, L, 6, N_KV_HEADS, HEAD_DIM)
    k_slc = rope(kv[:, :, 2], pos)
    k_win = rope(kv[:, :, 4], pos)
    rows = jnp.stack([kv[:, :, 0], kv[:, :, 1], k_slc, kv[:, :, 3]], axis=2)
    win = jnp.stack([k_win, kv[:, :, 5]], axis=2)
    return rows, win


def compress(seq, pe, w1, w2):
    L = seq.shape[1]
    nc = (L - CMP_LEN) // CMP_STRIDE + 1
    starts = jnp.arange(nc, dtype=jnp.int32) * CMP_STRIDE
    idx = starts[:, None] + jnp.arange(CMP_LEN, dtype=jnp.int32)[None]
    blk = seq[:, idx] + pe[None, None, :, None, :].astype(seq.dtype)
    hid = jax.nn.silu(jnp.einsum('bnlkd,ldf->bnkf', blk, w1))
    return hid @ w2, starts + CMP_LEN - 1


def cmp_to_slc(nc, ns):
    cs = jnp.arange(nc)[:, None] * CMP_STRIDE
    ss = jnp.arange(ns)[None] * SLC_BLOCK
    return ((cs < ss + SLC_BLOCK) & (cs + CMP_LEN > ss)).astype(jnp.float32)


def nsa_context(rows, cmp_pe, cmp_w1, cmp_w2):
    B, L = rows.shape[0], rows.shape[1]
    kc, c_end = compress(rows[:, :, 0], cmp_pe[0], cmp_w1[0], cmp_w2[0])
    vc, _ = compress(rows[:, :, 1], cmp_pe[1], cmp_w1[1], cmp_w2[1])
    ns = -(-L // SLC_BLOCK)
    slc = jnp.pad(rows[:, :, 2:4], ((0, 0), (0, ns * SLC_BLOCK - L), (0, 0), (0, 0), (0, 0)))
    slc = slc.reshape(B, ns, SLC_BLOCK, 2, N_KV_HEADS, HEAD_DIM).transpose(3, 0, 4, 1, 2, 5)
    return dict(kc=kc, vc=vc, c_end=c_end, ks=slc[0], vs=slc[1], overlap=cmp_to_slc(kc.shape[1], ns))


def nsa_attend_block(qn, qr, gates, pos, ctx, kw, vw, kw_pos):
    B, Q = qn.shape[0], qn.shape[1]
    dt = qn.dtype
    scale = HEAD_DIM ** -0.5
    s = jnp.einsum('bqkgd,bnkd->bkgqn', qn, ctx['kc']) * scale
    p_c = masked_softmax(s, ctx['c_end'][None, :] <= pos[:, None])
    o_c = jnp.einsum('bkgqn,bnkd->bqkgd', p_c.astype(dt), ctx['vc'])
    imp = jnp.einsum('bkgqn,nj->bkqj', p_c, ctx['overlap'])
    ns = ctx['overlap'].shape[1]
    blk = jnp.arange(ns)
    cur = pos // SLC_BLOCK
    valid = blk[None, :] * SLC_BLOCK <= pos[:, None]
    forced = (blk[None, :] == 0) | (blk[None, :] == cur[:, None]) | (blk[None, :] == cur[:, None] - 1)
    score = jnp.where(valid, imp + FORCE_BONUS * forced, -1.0)
    top, idx = lax.top_k(score, min(TOP_N, ns))
    bi = jnp.arange(B)[:, None, None, None]
    hi = jnp.arange(N_KV_HEADS)[None, :, None, None]
    kg = ctx['ks'][bi, hi, idx]
    vg = ctx['vs'][bi, hi, idx]
    s = jnp.einsum('bqkgd,bkqjrd->bkgqjr', qr, kg) * scale
    kpos = idx[..., None] * SLC_BLOCK + jnp.arange(SLC_BLOCK)
    m = (top >= 0)[..., None] & (kpos <= pos[None, None, :, None, None])
    kk = idx.shape[-1] * SLC_BLOCK
    p_s = masked_softmax(s.reshape(B, N_KV_HEADS, GROUP, Q, kk), m.reshape(B, N_KV_HEADS, 1, Q, kk)).reshape(s.shape)
    o_s = jnp.einsum('bkgqjr,bkqjrd->bqkgd', p_s.astype(dt), vg)
    s = jnp.einsum('bqkgd,blkd->bkgql', qr, kw) * scale
    rel = pos[:, None] - kw_pos[None, :]
    p_w = masked_softmax(s, (rel >= 0) & (rel < WINDOW) & (kw_pos[None, :] >= 0))
    o_w = jnp.einsum('bkgql,blkd->bqkgd', p_w.astype(dt), vw)
    return gates[..., 0:1] * o_c + gates[..., 1:2] * o_s + gates[..., 2:3] * o_w


def attend_prompt(qn, qr, gates, ctx, win):
    B, T = qn.shape[0], qn.shape[1]
    nb = T // Q_BLOCK
    kwp = jnp.pad(win, ((0, 0), (WINDOW, 0), (0, 0), (0, 0), (0, 0)))

    def to_blocks(a):
        return a.reshape((B, nb, Q_BLOCK) + a.shape[2:]).swapaxes(0, 1)

    def body(args):
        i, qn_b, qr_b, g_b = args
        start = i * Q_BLOCK
        pos = start + jnp.arange(Q_BLOCK, dtype=jnp.int32)
        kwb = lax.dynamic_slice_in_dim(kwp, start, WINDOW + Q_BLOCK, axis=1)
        kw_pos = start - WINDOW + jnp.arange(WINDOW + Q_BLOCK, dtype=jnp.int32)
        return nsa_attend_block(qn_b, qr_b, g_b, pos, ctx, kwb[:, :, 0], kwb[:, :, 1], kw_pos)

    o = lax.map(body, (jnp.arange(nb, dtype=jnp.int32), to_blocks(qn), to_blocks(qr), to_blocks(gates)))
    return o.swapaxes(0, 1).reshape(B, T, N_KV_HEADS, GROUP, HEAD_DIM)


def nsa_mixer(h, pos, w_qg, w_o, attend):
    B, L, _ = h.shape
    qg = h @ w_qg
    q = qg[..., :N_HEADS * HEAD_DIM].reshape(B, L, N_HEADS, HEAD_DIM)
    gates = jax.nn.sigmoid(qg[..., N_HEADS * HEAD_DIM:].astype(jnp.float32))
    gates = gates.reshape(B, L, N_KV_HEADS, GROUP, 3).astype(h.dtype)
    qr = rope(q, pos)
    shp = (B, L, N_KV_HEADS, GROUP, HEAD_DIM)
    o = attend(q.reshape(shp), qr.reshape(shp), gates)
    return o.reshape(B, L, N_HEADS * HEAD_DIM) @ w_o


def run_trunk(x, c, pos, conv_hist, p, build_kv):
    B = x.shape[0]
    cond = jax.nn.silu(c)
    conv_states = []
    attend = None
    kv_state = None
    for l in range(DEPTH):
        mod = (cond @ p['ada_w'][l] + p['ada_b'][l]).reshape(B, 1, N_MOD, D_MODEL)
        g = p['norm_g'][l]
        x = post(x, swiglu(pre(x, g[0], mod, 0), p['ffn_w_in'][l, 0], p['ffn_w_out'][l, 0]), g[1], mod, 0, MACARON_W)
        h = pre(x, g[2], mod, 1)
        if l < N_A_LAYERS:
            o, st = conv_mixer(h, conv_hist[l], p['conv_w_in'][l], p['conv_k'][l], p['conv_w_out'][l])
            conv_states.append(st)
        else:
            j = l - N_A_LAYERS
            o = nsa_mixer(h, pos, p['nsa_w_qg'][j], p['nsa_w_o'][j], attend)
        x = post(x, o, g[3], mod, 1, 1.0)
        x = post(x, swiglu(pre(x, g[4], mod, 2), p['ffn_w_in'][l, 1], p['ffn_w_out'][l, 1]), g[5], mod, 2, MACARON_W)
        if l == N_A_LAYERS - 1:
            attend, kv_state = build_kv(x)
    return x, kv_state, jnp.stack(conv_states)


def setup_inputs(seed: int = 0) -> dict:
    key = jax.random.key(seed)
    ks = jax.random.split(key, 24)
    f32 = jnp.float32

    def nrm(k, shape, s):
        return jax.random.normal(k, shape, f32) * s

    n_pages = PAST_LEN // PAGE_SIZE
    n_phys = (DEC_BATCH * n_pages * 5) // 4
    win_buf = min(WINDOW, PAST_LEN)
    perm = jax.random.permutation(ks[0], n_phys)
    page_table = perm[:DEC_BATCH * n_pages].reshape(DEC_BATCH, n_pages).astype(jnp.int32)
    qg_cols = N_HEADS * HEAD_DIM + 3 * N_HEADS
    return {
        "x_prompt": nrm(ks[1], (BATCH, SEQ, D_MODEL), 1.0),
        "x_sample": nrm(ks[2], (DEC_BATCH, DEC_SEQ, D_MODEL), 1.0),
        "cache_kv": nrm(ks[3], (n_phys, PAGE_SIZE, 4, N_KV_HEADS, HEAD_DIM), 1.0),
        "state_win_kv": nrm(ks[4], (DEC_BATCH, win_buf, 2, N_KV_HEADS, HEAD_DIM), 1.0),
        "state_conv": nrm(ks[5], (N_A_LAYERS, DEC_BATCH, CONV_W - 1, D_MODEL), 0.5),
        "page_table": page_table,
        "c_prompt": nrm(ks[6], (BATCH, D_MODEL), 1.0),
        "c_sample": nrm(ks[7], (DEC_BATCH, D_MODEL), 1.0),
        "ada_w": nrm(ks[8], (DEPTH, D_MODEL, N_MOD * D_MODEL), 0.5 * D_MODEL ** -0.5),
        "ada_b": nrm(ks[9], (DEPTH, N_MOD * D_MODEL), 0.01),
        "norm_g": 1.0 + nrm(ks[10], (DEPTH, 6, D_MODEL), 0.02),
        "ffn_w_in": nrm(ks[11], (DEPTH, 2, D_MODEL, 2 * D_FF), D_MODEL ** -0.5),
        "ffn_w_out": nrm(ks[12], (DEPTH, 2, D_FF, D_MODEL), D_FF ** -0.5),
        "conv_w_in": nrm(ks[13], (N_A_LAYERS, D_MODEL, 3 * D_MODEL), D_MODEL ** -0.5),
        "conv_k": nrm(ks[14], (N_A_LAYERS, CONV_W, D_MODEL), CONV_W ** -0.5),
        "conv_w_out": nrm(ks[15], (N_A_LAYERS, D_MODEL, D_MODEL), D_MODEL ** -0.5),
        "kv_norm_g": 1.0 + nrm(ks[16], (D_MODEL,), 0.02),
        "w_kv": nrm(ks[17], (D_MODEL, 6 * N_KV_HEADS * HEAD_DIM), D_MODEL ** -0.5),
        "cmp_pe": nrm(ks[18], (2, CMP_LEN, HEAD_DIM), 0.5),
        "cmp_w1": nrm(ks[19], (2, CMP_LEN, HEAD_DIM, CMP_HID), (CMP_LEN * HEAD_DIM) ** -0.5),
        "cmp_w2": nrm(ks[20], (2, CMP_HID, HEAD_DIM), CMP_HID ** -0.5),
        "nsa_w_qg": nrm(ks[21], (N_B_LAYERS, D_MODEL, qg_cols), D_MODEL ** -0.5),
        "nsa_w_o": nrm(ks[22], (N_B_LAYERS, N_HEADS * HEAD_DIM, D_MODEL), (N_HEADS * HEAD_DIM) ** -0.5),
    }


def reference(x_prompt, x_sample, cache_kv, state_win_kv, state_conv, page_table, c_prompt, c_sample,
              ada_w, ada_b, norm_g, ffn_w_in, ffn_w_out, conv_w_in, conv_k, conv_w_out,
              kv_norm_g, w_kv, cmp_pe, cmp_w1, cmp_w2, nsa_w_qg, nsa_w_o):
    p = dict(ada_w=ada_w, ada_b=ada_b, norm_g=norm_g, ffn_w_in=ffn_w_in, ffn_w_out=ffn_w_out,
             conv_w_in=conv_w_in, conv_k=conv_k, conv_w_out=conv_w_out,
             nsa_w_qg=nsa_w_qg, nsa_w_o=nsa_w_o)
    n_pages = page_table.shape[1]
    past_len = n_pages * PAGE_SIZE
    dec_b = x_sample.shape[0]
    pos_p = jnp.arange(x_prompt.shape[1], dtype=jnp.int32)
    pos_s = past_len + jnp.arange(x_sample.shape[1], dtype=jnp.int32)
    win_buf_p = min(WINDOW, x_prompt.shape[1])
    win_buf_s = state_win_kv.shape[1]

    def build_prompt(x):
        rows, win = shared_kv_rows(x, pos_p, kv_norm_g, w_kv)
        ctx = nsa_context(rows, cmp_pe, cmp_w1, cmp_w2)

        def attend(qn, qr, gates):
            return attend_prompt(qn, qr, gates, ctx, win)
        return attend, (rows, win[:, win.shape[1] - win_buf_p:])

    def build_sample(x):
        rows, win = shared_kv_rows(x, pos_s, kv_norm_g, w_kv)
        past = cache_kv[page_table].reshape(dec_b, past_len, 4, N_KV_HEADS, HEAD_DIM)
        ctx = nsa_context(jnp.concatenate([past, rows.astype(past.dtype)], axis=1), cmp_pe, cmp_w1, cmp_w2)
        kw = jnp.concatenate([state_win_kv, win.astype(state_win_kv.dtype)], axis=1)
        kw_pos = past_len - win_buf_s + jnp.arange(kw.shape[1], dtype=jnp.int32)

        def attend(qn, qr, gates):
            return nsa_attend_block(qn, qr, gates, pos_s, ctx, kw[:, :, 0], kw[:, :, 1], kw_pos)
        return attend, (rows, kw[:, kw.shape[1] - win_buf_s:])

    conv0 = jnp.zeros((N_A_LAYERS, x_prompt.shape[0], CONV_W - 1, D_MODEL), x_prompt.dtype)
    y_prompt, (kv_rows_p, win_p), conv_p = run_trunk(x_prompt, c_prompt, pos_p, conv0, p, build_prompt)
    y_sample, (kv_rows_s, win_s), conv_s = run_trunk(x_sample, c_sample, pos_s, state_conv, p, build_sample)
    return (y_prompt, y_sample, kv_rows_p, win_p, conv_p, kv_rows_s, win_s, conv_s)
```

```python
import functools

import jax
import jax.numpy as jnp
from jax import lax
from jax.experimental import pallas as pl
from jax.experimental.pallas import tpu as pltpu

F32 = jnp.float32
BF = jnp.bfloat16

HEAD_DIM = 64
N_HEADS = 16
N_KV_HEADS = 4
GROUP = N_HEADS // N_KV_HEADS
KV_W = N_KV_HEADS * HEAD_DIM
CMP_LEN = 32
CMP_STRIDE = 16
SLC_BLOCK = 64
TOP_N = 16
WINDOW = 512
Q_BLOCK = 128
CONV_W = 3
MACARON_W = 0.5
ROPE_THETA = 10000.0
EPS = 1e-6
FORCE_BONUS = 1e3
NEG = -1e30
N_MOD = 9
NS_PAD = 128
VMEM_LIMIT = 52 * 1024 * 1024


def _cp(sem):
    return pltpu.CompilerParams(dimension_semantics=sem, vmem_limit_bytes=VMEM_LIMIT)


def _dot(a, b):
    return jnp.dot(a, b, preferred_element_type=F32)


def _dot_nt(a, b):
    return lax.dot_general(a, b, (((1,), (1,)), ((), ())), preferred_element_type=F32)


def _split_dot(a, b_f32):
    hi = b_f32.astype(BF)
    lo = (b_f32 - hi.astype(F32)).astype(BF)
    return _dot(a, hi) + _dot(a, lo)


def _rms(x, g):
    return x * lax.rsqrt(jnp.mean(x * x, axis=-1, keepdims=True) + EPS) * g


def _silu(a):
    return a * jax.nn.sigmoid(a)


def _rope(x, cos, sin_signed):
    n = x.shape[-1]
    lane = lax.broadcasted_iota(jnp.int32, x.shape, 1)
    first = (lane & (HEAD_DIM // 2)) == 0
    partner = jnp.where(first, pltpu.roll(x, n - HEAD_DIM // 2, axis=1), pltpu.roll(x, HEAD_DIM // 2, axis=1))
    return x * cos + partner * sin_signed


def _ada_kernel(c_ref, w_ref, b_ref, o_ref):
    cond = _silu(c_ref[...]).astype(BF)
    o_ref[0, 0] = _dot(cond, w_ref[0].astype(BF)) + b_ref[0, 0]


def _ada_mod(c_all, ada_w, ada_b):
    depth, d, _ = ada_w.shape
    r = c_all.shape[0]
    return pl.pallas_call(
        _ada_kernel,
        grid=(depth, N_MOD),
        in_specs=[
            pl.BlockSpec((r, d), lambda l, j: (0, 0)),
            pl.BlockSpec((1, d, d), lambda l, j: (l, 0, j)),
            pl.BlockSpec((1, 1, 1, d), lambda l, j: (l, j, 0, 0)),
        ],
        out_specs=pl.BlockSpec((1, 1, r, d), lambda l, j: (l, j, 0, 0)),
        out_shape=jax.ShapeDtypeStruct((depth, N_MOD, r, d), F32),
        compiler_params=_cp(("parallel", "parallel")),
        name="ada_mod",
    )(c_all, ada_w, ada_b.reshape(depth, N_MOD, 1, d))


def _row_specs(bx, tm, d, mod, layer, sub):
    lm = mod.shape[3]
    tml = 1 if lm == 1 else tm
    x_spec = pl.BlockSpec((1, tm, d), lambda b, i, *_: (b, i, 0))
    mod_spec = pl.BlockSpec(
        (None, 3, None, tml, d), lambda b, i, *_: (layer, sub, b, 0 if lm == 1 else i, 0)
    )
    g_spec = pl.BlockSpec((None, 6, d), lambda b, i, *_: (layer, 0, 0))
    return x_spec, mod_spec, g_spec


def _pre(x, g_ref, m, sub):
    return _rms(x, g_ref[2 * sub : 2 * sub + 1, :]) * (1.0 + m[1]) + m[0]


def _post(x, o, g_ref, m, sub, w):
    return x + w * m[2] * _rms(o, g_ref[2 * sub + 1 : 2 * sub + 2, :])


def _ffn_kernel(x_ref, mod_ref, g_ref, wa_ref, wb_ref, wo_ref, y_ref, h_scr, acc_scr, *, sub, nf):
    f = pl.program_id(2)

    @pl.when(f == 0)
    def _():
        h_scr[...] = _pre(x_ref[0], g_ref, mod_ref[...], sub).astype(BF)

    h = h_scr[...]
    u = _silu(_dot(h, wa_ref[...])) * _dot(h, wb_ref[...])
    part = _dot(u.astype(BF), wo_ref[...])

    @pl.when(f == 0)
    def _():
        acc_scr[...] = part

    @pl.when(f > 0)
    def _():
        acc_scr[...] += part

    @pl.when(f == nf - 1)
    def _():
        y_ref[0] = _post(x_ref[0], acc_scr[...], g_ref, mod_ref[...], sub, MACARON_W)


def _ffn_block(x, mod, norm_g, w_in, w_out, layer, sub, half, tm, tf):
    bx, L, d = x.shape
    dff = w_out.shape[2]
    nf = dff // tf
    x_spec, mod_spec, g_spec = _row_specs(bx, tm, d, mod, layer, sub)
    return pl.pallas_call(
        functools.partial(_ffn_kernel, sub=sub, nf=nf),
        grid=(bx, L // tm, nf),
        in_specs=[
            x_spec,
            mod_spec,
            g_spec,
            pl.BlockSpec((None, None, d, tf), lambda b, i, f: (layer, half, 0, f)),
            pl.BlockSpec((None, None, d, tf), lambda b, i, f: (layer, half, 0, nf + f)),
            pl.BlockSpec((None, None, tf, d), lambda b, i, f: (layer, half, f, 0)),
        ],
        out_specs=pl.BlockSpec((1, tm, d), lambda b, i, f: (b, i, 0)),
        out_shape=jax.ShapeDtypeStruct((bx, L, d), F32),
        scratch_shapes=[pltpu.VMEM((tm, d), BF), pltpu.VMEM((tm, d), F32)],
        compiler_params=_cp(("parallel", "parallel", "arbitrary")),
        name="ffn_block",
    )(x, mod, norm_g, w_in, w_in, w_out)


def _conv_kernel(x_ref, mod_ref, g_ref, win_ref, ck_ref, wout_ref, p0_ref, p1_ref, y_ref, u_ref, u_scr, *, seq_mode, tm):
    d = x_ref.shape[-1]
    x = x_ref[0]
    m = mod_ref[...]
    h = _pre(x, g_ref, m, 1).astype(BF)
    z = _dot(h, win_ref[...])
    bg = z[:, 0:d]
    u = z[:, d : 2 * d] * z[:, 2 * d : 3 * d]
    ck = ck_ref[...]
    if seq_mode:
        @pl.when(pl.program_id(1) == 0)
        def _():
            u_scr[6:7, :] = p0_ref[0]
            u_scr[7:8, :] = p1_ref[0]

        u_scr[8 : 8 + tm, :] = u
        prev2 = u_scr[6 : 6 + tm, :]
        prev1 = u_scr[7 : 7 + tm, :]
        u_scr[6:8, :] = u[tm - 2 : tm, :]
        u_ref[0] = u[tm - 2 : tm, :]
    else:
        prev2 = p0_ref[0]
        prev1 = p1_ref[0]
        u_ref[0] = u
    y = ck[0:1, :] * prev2 + ck[1:2, :] * prev1 + ck[2:3, :] * u
    o = _dot((bg * y).astype(BF), wout_ref[...])
    y_ref[0] = _post(x, o, g_ref, m, 1, 1.0)


def _conv_block(x, mod, norm_g, w_in, conv_k, w_out, prev0, prev1, layer, tm, seq_mode):
    bx, L, d = x.shape
    x_spec, mod_spec, g_spec = _row_specs(bx, tm, d, mod, layer, 1)
    if seq_mode:
        p_spec = pl.BlockSpec((1, 1, d), lambda b, i: (b, 0, 0))
        u_spec = pl.BlockSpec((1, 2, d), lambda b, i: (b, 0, 0))
        u_shape = jax.ShapeDtypeStruct((bx, 2, d), F32)
    else:
        p_spec = pl.BlockSpec((1, tm, d), lambda b, i: (b, i, 0))
        u_spec = pl.BlockSpec((1, tm, d), lambda b, i: (b, i, 0))
        u_shape = jax.ShapeDtypeStruct((bx, L, d), F32)
    return pl.pallas_call(
        functools.partial(_conv_kernel, seq_mode=seq_mode, tm=tm),
        grid=(bx, L // tm),
        in_specs=[
            x_spec,
            mod_spec,
            g_spec,
            pl.BlockSpec((None, d, 3 * d), lambda b, i: (layer, 0, 0)),
            pl.BlockSpec((None, CONV_W, d), lambda b, i: (layer, 0, 0)),
            pl.BlockSpec((None, d, d), lambda b, i: (layer, 0, 0)),
            p_spec,
            p_spec,
        ],
        out_specs=[pl.BlockSpec((1, tm, d), lambda b, i: (b, i, 0)), u_spec],
        out_shape=[jax.ShapeDtypeStruct((bx, L, d), F32), u_shape],
        scratch_shapes=[pltpu.VMEM((tm + 8, d), F32)],
        compiler_params=_cp(("parallel", "arbitrary")),
        name="conv_block",
    )(x, mod, norm_g, w_in, conv_k, w_out, prev0, prev1)


def _kv_kernel(x_ref, g_ref, w_ref, cos_ref, sin_ref, rows_ref, win_ref):
    hn = _rms(x_ref[0], g_ref[...]).astype(BF)
    kv = _dot(hn, w_ref[...])
    cos = cos_ref[...]
    sin = sin_ref[...]
    rows_ref[0, :, 0 : 2 * KV_W] = kv[:, 0 : 2 * KV_W]
    rows_ref[0, :, 2 * KV_W : 3 * KV_W] = _rope(kv[:, 2 * KV_W : 3 * KV_W], cos, sin)
    rows_ref[0, :, 3 * KV_W : 4 * KV_W] = kv[:, 3 * KV_W : 4 * KV_W]
    win_ref[0, :, 0:KV_W] = _rope(kv[:, 4 * KV_W : 5 * KV_W], cos, sin)
    win_ref[0, :, KV_W : 2 * KV_W] = kv[:, 5 * KV_W : 6 * KV_W]


def _kv_proj(x, kv_norm_g, w_kv, cos, sin, tm):
    bx, L, d = x.shape
    return pl.pallas_call(
        _kv_kernel,
        grid=(bx, L // tm),
        in_specs=[
            pl.BlockSpec((1, tm, d), lambda b, i: (b, i, 0)),
            pl.BlockSpec((1, d), lambda b, i: (0, 0)),
            pl.BlockSpec((d, 6 * KV_W), lambda b, i: (0, 0)),
            pl.BlockSpec((tm, KV_W), lambda b, i: (i, 0)),
            pl.BlockSpec((tm, KV_W), lambda b, i: (i, 0)),
        ],
        out_specs=[
            pl.BlockSpec((1, tm, 4 * KV_W), lambda b, i: (b, i, 0)),
            pl.BlockSpec((1, tm, 2 * KV_W), lambda b, i: (b, i, 0)),
        ],
        out_shape=[
            jax.ShapeDtypeStruct((bx, L, 4 * KV_W), F32),
            jax.ShapeDtypeStruct((bx, L, 2 * KV_W), F32),
        ],
        compiler_params=_cp(("parallel", "parallel")),
        name="kv_proj",
    )(x, kv_norm_g.reshape(1, d), w_kv, cos, sin)


def _qg_kernel(x_ref, mod_ref, g_ref, wq_ref, wg_ref, cos_ref, sin_ref, qn_ref, qr_ref, gate_ref):
    h = _pre(x_ref[0], g_ref, mod_ref[...], 1).astype(BF)
    q = _dot(h, wq_ref[...])
    gate_ref[0] = jax.nn.sigmoid(_dot(h, wg_ref[...]))
    scale = HEAD_DIM**-0.5
    qn_ref[0] = (q * scale).astype(BF)
    cos = cos_ref[...]
    sin = sin_ref[...]
    for c in range(q.shape[1] // KV_W):
        sl = slice(c * KV_W, (c + 1) * KV_W)
        qr_ref[0, :, sl] = (_rope(q[:, sl], cos, sin) * scale).astype(BF)


def _qg_proj(x, mod, norm_g, wq, wg, cos, sin, layer, tm):
    bx, L, d = x.shape
    nq = wq.shape[-1]
    x_spec, mod_spec, g_spec = _row_specs(bx, tm, d, mod, layer, 1)
    return pl.pallas_call(
        _qg_kernel,
        grid=(bx, L // tm),
        in_specs=[
            x_spec,
            mod_spec,
            g_spec,
            pl.BlockSpec((d, nq), lambda b, i: (0, 0)),
            pl.BlockSpec((d, 128), lambda b, i: (0, 0)),
            pl.BlockSpec((tm, KV_W), lambda b, i: (i, 0)),
            pl.BlockSpec((tm, KV_W), lambda b, i: (i, 0)),
        ],
        out_specs=[
            pl.BlockSpec((1, tm, nq), lambda b, i: (b, i, 0)),
            pl.BlockSpec((1, tm, nq), lambda b, i: (b, i, 0)),
            pl.BlockSpec((1, tm, 128), lambda b, i: (b, i, 0)),
        ],
        out_shape=[
            jax.ShapeDtypeStruct((bx, L, nq), BF),
            jax.ShapeDtypeStruct((bx, L, nq), BF),
            jax.ShapeDtypeStruct((bx, L, 128), F32),
        ],
        compiler_params=_cp(("parallel", "parallel")),
        name="qg_proj",
    )(x, mod, norm_g, wq, wg, cos, sin)


def _op_kernel(o_ref, x_ref, mod_ref, g_ref, w_ref, y_ref):
    a = _dot(o_ref[0], w_ref[...])
    y_ref[0] = _post(x_ref[0], a, g_ref, mod_ref[...], 1, 1.0)


def _out_proj(o, x, mod, norm_g, w_o, layer, tm):
    bx, L, d = x.shape
    x_spec, mod_spec, g_spec = _row_specs(bx, tm, d, mod, layer, 1)
    return pl.pallas_call(
        _op_kernel,
        grid=(bx, L // tm),
        in_specs=[
            pl.BlockSpec((1, tm, o.shape[-1]), lambda b, i: (b, i, 0)),
            x_spec,
            mod_spec,
            g_spec,
            pl.BlockSpec(w_o.shape, lambda b, i: (0, 0)),
        ],
        out_specs=pl.BlockSpec((1, tm, d), lambda b, i: (b, i, 0)),
        out_shape=jax.ShapeDtypeStruct((bx, L, d), F32),
        compiler_params=_cp(("parallel", "parallel")),
        name="out_proj",
    )(o, x, mod, norm_g, w_o)


def _compress_tile(get_x, pe_ref, w1_ref, w2_ref, b_scr, m_rows):
    half = CMP_LEN // 2
    xs = [get_x(l) for l in range(half)]
    xa = jnp.concatenate([(xs[l] + pe_ref[l]).astype(BF) for l in range(half)], axis=1)
    xb = jnp.concatenate([(xs[l] + pe_ref[half + l]).astype(BF) for l in range(half)], axis=1)
    a = _dot(xa, w1_ref[0])
    b_scr[0:m_rows, :] = _dot(xb, w1_ref[1])
    b_scr[m_rows : m_rows + 8, :] = jnp.zeros((8, b_scr.shape[1]), F32)
    hid = a + b_scr[1 : m_rows + 1, :]
    return _dot(_silu(hid).astype(BF), w2_ref[...])


def _cmp_prompt_kernel(x_ref, pe_ref, w1_ref, w2_ref, o_ref, b_scr, *, ng, nc):
    row = lax.broadcasted_iota(jnp.int32, (ng, 128), 0)
    res = _compress_tile(
        lambda l: x_ref[0, pl.ds(l, ng, stride=CMP_STRIDE), :], pe_ref, w1_ref, w2_ref, b_scr, ng
    )
    o_ref[0] = jnp.where(row < nc, res, 0.0)


def _cmp_weights(cmp_pe, cmp_w1, cmp_w2):
    z1 = jnp.zeros_like(cmp_w1)
    w1bd = jnp.concatenate(
        [jnp.concatenate([cmp_w1, z1], axis=3), jnp.concatenate([z1, cmp_w1], axis=3)], axis=2
    )
    hid2 = w1bd.shape[-1]
    w1bd = w1bd.reshape(2, 2, (CMP_LEN // 2) * 2 * HEAD_DIM, hid2).astype(BF)
    z2 = jnp.zeros_like(cmp_w2)
    w2bd = jnp.concatenate(
        [jnp.concatenate([cmp_w2, z2], axis=2), jnp.concatenate([z2, cmp_w2], axis=2)], axis=1
    ).astype(BF)
    pe2 = jnp.concatenate([cmp_pe, cmp_pe], axis=2)[:, :, None, :]
    return pe2, w1bd, w2bd


def _compress_prompt(rows, pe2, w1bd, w2bd):
    bx, L, _ = rows.shape
    ng = L // CMP_STRIDE
    nc = (L - CMP_LEN) // CMP_STRIDE + 1
    kdim, hid2 = w1bd.shape[2], w1bd.shape[3]
    return pl.pallas_call(
        functools.partial(_cmp_prompt_kernel, ng=ng, nc=nc),
        grid=(4, bx),
        in_specs=[
            pl.BlockSpec((1, L, 128), lambda cg, b: (b, 0, cg)),
            pl.BlockSpec((None, CMP_LEN, 1, 128), lambda cg, b: (cg // 2, 0, 0, 0)),
            pl.BlockSpec((None, 2, kdim, hid2), lambda cg, b: (cg // 2, 0, 0, 0)),
            pl.BlockSpec((None, hid2, 128), lambda cg, b: (cg // 2, 0, 0)),
        ],
        out_specs=pl.BlockSpec((1, ng, 128), lambda cg, b: (b, 0, cg)),
        out_shape=jax.ShapeDtypeStruct((bx, ng, 2 * KV_W), F32),
        scratch_shapes=[pltpu.VMEM((ng + 8, hid2), F32)],
        compiler_params=_cp(("parallel", "parallel")),
        name="compress_prompt",
    )(rows, pe2, w1bd, w2bd)


def _cmp_sample_kernel(pt_ref, *refs, n_pages, sb, gpp, nc):
    del pt_ref
    page_refs = refs[: sb * n_pages]
    pe_ref, w1_ref, w2_ref, o_ref, b_scr = refs[sb * n_pages :]
    ng = n_pages * gpp
    m_rows = sb * ng
    row = lax.broadcasted_iota(jnp.int32, (ng, 128), 0)

    def get_x(l):
        return jnp.concatenate([r[0, pl.ds(l, gpp, stride=CMP_STRIDE), :] for r in page_refs], axis=0)

    res = _compress_tile(get_x, pe_ref, w1_ref, w2_ref, b_scr, m_rows)
    for s in range(sb):
        o_ref[s] = jnp.where(row < nc, res[s * ng : (s + 1) * ng, :], 0.0)


def _compress_sample(cache2, page_table, pe2, w1bd, w2bd, total_len, sb):
    db, n_pages = page_table.shape
    page = cache2.shape[1]
    gpp = page // CMP_STRIDE
    ng = n_pages * gpp
    nc = (total_len - CMP_LEN) // CMP_STRIDE + 1
    assert nc <= ng
    kdim, hid2 = w1bd.shape[2], w1bd.shape[3]

    def page_spec(s, p):
        return pl.BlockSpec((1, page, 128), lambda cg, bb, pt: (pt[bb * sb + s, p], 0, cg))

    grid_spec = pltpu.PrefetchScalarGridSpec(
        num_scalar_prefetch=1,
        grid=(4, db // sb),
        in_specs=[page_spec(s, p) for s in range(sb) for p in range(n_pages)]
        + [
            pl.BlockSpec((None, CMP_LEN, 1, 128), lambda cg, bb, pt: (cg // 2, 0, 0, 0)),
            pl.BlockSpec((None, 2, kdim, hid2), lambda cg, bb, pt: (cg // 2, 0, 0, 0)),
            pl.BlockSpec((None, hid2, 128), lambda cg, bb, pt: (cg // 2, 0, 0)),
        ],
        out_specs=pl.BlockSpec((sb, ng, 128), lambda cg, bb, pt: (bb, 0, cg)),
        scratch_shapes=[pltpu.VMEM((sb * ng + 8, hid2), F32)],
    )
    return pl.pallas_call(
        functools.partial(_cmp_sample_kernel, n_pages=n_pages, sb=sb, gpp=gpp, nc=nc),
        grid_spec=grid_spec,
        out_shape=jax.ShapeDtypeStruct((db, ng, 2 * KV_W), F32),
        compiler_params=_cp(("parallel", "parallel")),
        name="compress_sample",
    )(page_table, *([cache2] * (sb * n_pages)), pe2, w1bd, w2bd)


def _select_scores(imp, blk, pos):
    cur = pos // SLC_BLOCK
    valid = blk * SLC_BLOCK <= pos
    forced = (blk == 0) | (blk == cur) | (blk == cur - 1)
    return jnp.where(valid, imp + FORCE_BONUS * forced.astype(F32), -1.0)


def _topk_rows(score, k):
    n = score.shape[0]
    ridx = lax.broadcasted_iota(jnp.int32, score.shape, 0).astype(F32)

    def body(_, carry):
        sc, sel = carry
        m = jnp.max(sc, axis=0, keepdims=True)
        first = jnp.min(jnp.where(sc == m, ridx, float(n)), axis=0, keepdims=True)
        hit = ridx == first
        return jnp.where(hit, -jnp.inf, sc), jnp.where(hit, 1.0, sel)

    _, sel = lax.fori_loop(0, k, body, (score, jnp.zeros_like(score)), unroll=True)
    return sel > 0.5


def _softmax_cols(s, mask):
    sm = jnp.where(mask, s, NEG)
    e = jnp.exp(sm - jnp.max(sm, axis=0, keepdims=True)) * mask.astype(F32)
    return e / jnp.maximum(jnp.sum(e, axis=0, keepdims=True), 1e-30)


def _attn_prompt_kernel(
    qn_ref, qr_ref, g_ref, kc_ref, vct_ref, ks_ref, vst_ref, kw_ref, vwt_ref, ovt_ref,
    o_ref, bias_scr, m_scr, l_scr, acc_scr, *, tq, tk, nc, ns, top_n,
):
    start = pl.program_id(2) * tq
    gq = GROUP * tq

    def heads_on_lanes(ref):
        t = ref[0]
        return jnp.concatenate([t[HEAD_DIM * g : HEAD_DIM * (g + 1), :] for g in range(GROUP)], axis=1)

    qn = heads_on_lanes(qn_ref)
    qr = heads_on_lanes(qr_ref)
    pos = start + lax.broadcasted_iota(jnp.int32, (1, tq), 1)
    pos_g = jnp.concatenate([pos] * GROUP, axis=1)

    ngp = kc_ref.shape[2]
    n_idx = lax.broadcasted_iota(jnp.int32, (ngp, 1), 0)
    cmask = (n_idx * CMP_STRIDE + (CMP_LEN - 1) <= pos_g) & (n_idx < nc)
    p_c = _softmax_cols(_dot(kc_ref[0, 0], qn), cmask)
    o_c = _dot(vct_ref[0], p_c.astype(BF))
    psum = p_c[:, 0:tq]
    for g in range(1, GROUP):
        psum = psum + p_c[:, g * tq : (g + 1) * tq]
    imp = _split_dot(ovt_ref[...], psum)

    blk = lax.broadcasted_iota(jnp.int32, (NS_PAD, 1), 0)
    score = jnp.where(blk < ns, _select_scores(imp, blk, pos), -jnp.inf)
    sel = _topk_rows(score, top_n) & (score >= 0.0)
    bias_scr[...] = jnp.where(sel, 0.0, NEG)

    m_scr[...] = jnp.full(m_scr.shape, NEG, F32)
    l_scr[...] = jnp.zeros(l_scr.shape, F32)
    acc_scr[...] = jnp.zeros(acc_scr.shape, F32)
    bpc = tk // SLC_BLOCK

    def chunk(c, carry):
        k0 = pl.multiple_of(c * tk, tk)
        s = _dot(ks_ref[0, 0, pl.ds(k0, tk), :], qr)
        b = jnp.concatenate(
            [jnp.broadcast_to(bias_scr[pl.ds(c * bpc + j, 1), :], (SLC_BLOCK, tq)) for j in range(bpc)], axis=0
        )
        kpos = k0 + lax.broadcasted_iota(jnp.int32, (tk, 1), 0)
        b = jnp.where(kpos <= pos, b, NEG)
        s = s + jnp.concatenate([b] * GROUP, axis=1)
        m_old = m_scr[...]
        m_new = jnp.maximum(m_old, jnp.max(s, axis=0, keepdims=True))
        alpha = jnp.exp(m_old - m_new)
        p = jnp.exp(s - m_new)
        l_scr[...] = alpha * l_scr[...] + jnp.sum(p, axis=0, keepdims=True)
        acc_scr[...] = alpha * acc_scr[...] + _dot(vst_ref[0, :, pl.ds(k0, tk)], p.astype(BF))
        m_scr[...] = m_new
        return carry

    lax.fori_loop(0, (start + tq - 1) // tk + 1, chunk, 0)
    o_s = acc_scr[...] / jnp.maximum(l_scr[...], 1e-30)

    wk = WINDOW + tq
    w0 = pl.multiple_of(jnp.maximum(start - WINDOW, 0), 128)
    rel = pos_g - (w0 + lax.broadcasted_iota(jnp.int32, (wk, 1), 0))
    p_w = _softmax_cols(_dot(kw_ref[0, 0, pl.ds(w0, wk), :], qr), (rel >= 0) & (rel < WINDOW))
    o_w = _dot(vwt_ref[0, :, pl.ds(w0, wk)], p_w.astype(BF))

    gates = g_ref[0, 0]
    outs = []
    for g in range(GROUP):
        sl = slice(g * tq, (g + 1) * tq)
        outs.append(
            gates[3 * g : 3 * g + 1, :] * o_c[:, sl]
            + gates[3 * g + 1 : 3 * g + 2, :] * o_s[:, sl]
            + gates[3 * g + 2 : 3 * g + 3, :] * o_w[:, sl]
        )
    o_ref[0] = jnp.concatenate(outs, axis=0).astype(BF)


def _attn_prompt(qn_t, qr_t, gates_t, kc, vc_t, ks, vs_t, kw, vw_t, ov_t, nc, ns, tq, tk):
    bx, _, L = qn_t.shape
    ngp = kc.shape[2]
    gd = GROUP * HEAD_DIM
    kv_spec = pl.BlockSpec((1, 1, L, HEAD_DIM), lambda b, k, i: (b, k, 0, 0))
    vt_spec = pl.BlockSpec((1, HEAD_DIM, L), lambda b, k, i: (b, k, 0))
    q_spec = pl.BlockSpec((1, gd, tq), lambda b, k, i: (b, k, i))
    return pl.pallas_call(
        functools.partial(_attn_prompt_kernel, tq=tq, tk=tk, nc=nc, ns=ns, top_n=min(TOP_N, ns)),
        grid=(bx, N_KV_HEADS, L // tq),
        in_specs=[
            q_spec,
            q_spec,
            pl.BlockSpec((1, 1, 16, tq), lambda b, k, i: (b, k, 0, i)),
            pl.BlockSpec((1, 1, ngp, HEAD_DIM), lambda b, k, i: (b, k, 0, 0)),
            pl.BlockSpec((1, HEAD_DIM, ngp), lambda b, k, i: (b, k, 0)),
            kv_spec,
            vt_spec,
            kv_spec,
            vt_spec,
            pl.BlockSpec(ov_t.shape, lambda b, k, i: (0, 0)),
        ],
        out_specs=q_spec,
        out_shape=jax.ShapeDtypeStruct(qn_t.shape, BF),
        scratch_shapes=[
            pltpu.VMEM((NS_PAD, tq), F32),
            pltpu.VMEM((1, GROUP * tq), F32),
            pltpu.VMEM((1, GROUP * tq), F32),
            pltpu.VMEM((HEAD_DIM, GROUP * tq), F32),
        ],
        compiler_params=_cp(("parallel", "parallel", "arbitrary")),
        name="attn_prompt",
    )(qn_t, qr_t, gates_t, kc, vc_t, ks, vs_t, kw, vw_t, ov_t)


def _softmax_rows_ext(s, mask, s_new, m_new_mask):
    sm = jnp.where(mask, s, NEG)
    sn = jnp.where(m_new_mask, s_new, NEG)
    mx = jnp.maximum(jnp.max(sm, axis=1, keepdims=True), sn)
    e = jnp.exp(sm - mx) * mask.astype(F32)
    en = jnp.exp(sn - mx) * m_new_mask.astype(F32)
    inv = 1.0 / jnp.maximum(jnp.sum(e, axis=1, keepdims=True) + en, 1e-30)
    return e * inv, en * inv


def _attn_sample_kernel(
    pt_ref, *refs, n_pages, page, past_len, nc, ns, top_n, win_buf,
):
    del pt_ref
    page_refs = refs[:n_pages]
    (qn_ref, qr_ref, g_ref, cmp_ref, new_ref, win_ref, wnew_ref, ov_ref, exp_ref, gsum_ref, o_ref) = refs[n_pages:]
    pos = past_len
    qn = qn_ref[0]
    qr = qr_ref[0]
    qr32 = qr.astype(F32)
    nh = qn.shape[0]

    cmp = cmp_ref[0]
    kc = cmp[:, 0:KV_W].astype(BF)
    vc = cmp[:, KV_W : 2 * KV_W].astype(BF)
    ngp = kc.shape[0]
    n_idx = lax.broadcasted_iota(jnp.int32, (1, ngp), 1)
    cmask = jnp.broadcast_to((n_idx * CMP_STRIDE + (CMP_LEN - 1) <= pos) & (n_idx < nc), (nh, ngp))
    sm = jnp.where(cmask, _dot_nt(qn, kc), NEG)
    e = jnp.exp(sm - jnp.max(sm, axis=1, keepdims=True)) * cmask.astype(F32)
    p_c = e / jnp.maximum(jnp.sum(e, axis=1, keepdims=True), 1e-30)
    o_c = _dot(p_c.astype(BF), vc)

    t = _split_dot_r(p_c, ov_ref[...])
    imp = _split_dot(gsum_ref[...], t)
    blk = lax.broadcasted_iota(jnp.int32, (1, NS_PAD), 1)
    score = jnp.where(blk < ns, _select_scores(imp, blk, pos), -jnp.inf)
    pad = jnp.concatenate([score, jnp.zeros((NS_PAD - nh, NS_PAD), F32)], axis=0)
    score_t = pad.T
    ii = lax.broadcasted_iota(jnp.int32, (NS_PAD, NS_PAD), 0)
    jj = lax.broadcasted_iota(jnp.int32, (NS_PAD, NS_PAD), 1)
    head_row = lax.broadcasted_iota(jnp.int32, (nh, NS_PAD), 0)
    sel = jnp.zeros((nh, NS_PAD), F32)
    for h in range(nh):
        a = score_t[:, h : h + 1]
        b = score[h : h + 1, :]
        beats = (a > b) | ((a == b) & (ii < jj))
        rank = jnp.sum(beats.astype(F32), axis=0, keepdims=True)
        sel_h = jnp.where((rank < float(top_n)) & (b >= 0.0), 1.0, 0.0)
        sel = jnp.where(head_row == h, sel_h, sel)

    kmask = _dot(sel.astype(BF), exp_ref[...]) > 0.5
    new = new_ref[0]
    k_new = new[:, 2 * KV_W : 3 * KV_W]
    v_new = new[:, 3 * KV_W : 4 * KV_W]
    s_new = jnp.sum(qr32 * k_new.astype(BF).astype(F32), axis=1, keepdims=True)
    nb = pos // SLC_BLOCK
    new_sel = sel[:, nb : nb + 1] > 0.5
    s = jnp.concatenate([_dot_nt(qr, r[0][:, 0:KV_W].astype(BF)) for r in page_refs], axis=1)
    p_s, p_new = _softmax_rows_ext(s, kmask, s_new, new_sel)
    o_s = p_new * v_new.astype(BF).astype(F32)
    for i, r in enumerate(page_refs):
        o_s = o_s + _dot(p_s[:, i * page : (i + 1) * page].astype(BF), r[0][:, KV_W : 2 * KV_W].astype(BF))

    wbuf = win_ref[0]
    wnew = wnew_ref[0]
    kw_pos = (past_len - win_buf) + lax.broadcasted_iota(jnp.int32, (1, win_buf), 1)
    rel = pos - kw_pos
    wmask = jnp.broadcast_to((rel >= 0) & (rel < WINDOW) & (kw_pos >= 0), (nh, win_buf))
    s_w = _dot_nt(qr, wbuf[:, 0:KV_W].astype(BF))
    s_wn = jnp.sum(qr32 * wnew[:, 0:KV_W].astype(BF).astype(F32), axis=1, keepdims=True)
    p_w, p_wn = _softmax_rows_ext(s_w, wmask, s_wn, jnp.full((nh, 1), True))
    o_w = _dot(p_w.astype(BF), wbuf[:, KV_W : 2 * KV_W].astype(BF)) + p_wn * wnew[:, KV_W : 2 * KV_W].astype(BF).astype(F32)

    gates = g_ref[0]
    o_ref[0] = gates[:, 0:1] * o_c + gates[:, 1:2] * o_s + gates[:, 2:3] * o_w


def _split_dot_r(a_f32, b):
    hi = a_f32.astype(BF)
    lo = (a_f32 - hi.astype(F32)).astype(BF)
    return _dot(hi, b) + _dot(lo, b)


def _attn_sample(cache2, page_table, qn_bd, qr_bd, gates, cmp_s, rows_new, win_state, win_new, ov, expand, gsum, nc, ns):
    db, n_pages = page_table.shape
    page = cache2.shape[1]
    past_len = n_pages * page
    win_buf = win_state.shape[1]
    ngp = cmp_s.shape[1]

    def one(shape):
        return pl.BlockSpec((1,) + shape, lambda b, pt: (b, 0, 0))

    def const(a):
        return pl.BlockSpec(a.shape, lambda b, pt: (0, 0))

    grid_spec = pltpu.PrefetchScalarGridSpec(
        num_scalar_prefetch=1,
        grid=(db,),
        in_specs=[pl.BlockSpec((1, page, 2 * KV_W), (lambda b, pt, p=p: (pt[b, p], 0, 1))) for p in range(n_pages)]
        + [
            one((N_HEADS, KV_W)),
            one((N_HEADS, KV_W)),
            one((N_HEADS, 3)),
            one((ngp, 2 * KV_W)),
            one((1, 4 * KV_W)),
            one((win_buf, 2 * KV_W)),
            one((1, 2 * KV_W)),
            const(ov),
            const(expand),
            const(gsum),
        ],
        out_specs=one((N_HEADS, KV_W)),
    )
    return pl.pallas_call(
        functools.partial(
            _attn_sample_kernel, n_pages=n_pages, page=page, past_len=past_len, nc=nc, ns=ns,
            top_n=min(TOP_N, ns), win_buf=win_buf,
        ),
        grid_spec=grid_spec,
        out_shape=jax.ShapeDtypeStruct((db, N_HEADS, KV_W), F32),
        compiler_params=_cp(("parallel",)),
        name="attn_sample",
    )(page_table, *([cache2] * n_pages), qn_bd, qr_bd, gates, cmp_s, rows_new, win_state, win_new, ov, expand, gsum)


def _rope_tables(pos):
    half = HEAD_DIM // 2
    inv = ROPE_THETA ** (-jnp.arange(half, dtype=F32) / half)
    ang = pos.astype(F32)[:, None] * inv[None]
    cos, sin = jnp.cos(ang), jnp.sin(ang)
    cos = jnp.tile(jnp.concatenate([cos, cos], axis=1), (1, N_KV_HEADS))
    sin = jnp.tile(jnp.concatenate([-sin, sin], axis=1), (1, N_KV_HEADS))
    return cos, sin


def _overlap(n_rows, n_cols):
    cs = jnp.arange(n_rows)[:, None] * CMP_STRIDE
    ss = jnp.arange(n_cols)[None] * SLC_BLOCK
    return ((cs < ss + SLC_BLOCK) & (cs + CMP_LEN > ss)).astype(F32)


def _pick_tile(n, pref):
    t = min(n, pref)
    while n % t:
        t //= 2
    return t


def kernel(x_prompt, x_sample, cache_kv, state_win_kv, state_conv, page_table, c_prompt, c_sample,
           ada_w, ada_b, norm_g, ffn_w_in, ffn_w_out, conv_w_in, conv_k, conv_w_out,
           kv_norm_g, w_kv, cmp_pe, cmp_w1, cmp_w2, nsa_w_qg, nsa_w_o):
    bp, seq, d = x_prompt.shape
    db, dec_seq, _ = x_sample.shape
    assert dec_seq == 1
    depth = ada_w.shape[0]
    n_a = conv_w_in.shape[0]
    n_pages = page_table.shape[1]
    page = cache_kv.shape[1]
    past_len = n_pages * page
    win_buf = state_win_kv.shape[1]
    dff = ffn_w_out.shape[2]
    nq = N_HEADS * HEAD_DIM

    ffn_w_in_b = ffn_w_in.astype(BF)
    ffn_w_out_b = ffn_w_out.astype(BF)
    conv_w_in_b = conv_w_in.astype(BF)
    conv_w_out_b = conv_w_out.astype(BF)
    w_kv_b = w_kv.astype(BF)
    wq_b = nsa_w_qg[:, :, :nq].astype(BF)
    wg_b = jnp.pad(nsa_w_qg[:, :, nq:], ((0, 0), (0, 0), (0, 128 - 3 * N_HEADS))).astype(BF)
    w_o_b = nsa_w_o.astype(BF)
    pe2, w1bd, w2bd = _cmp_weights(cmp_pe, cmp_w1, cmp_w2)

    r_pad = -(db + bp) % 8
    c_all = jnp.concatenate([c_sample, c_prompt, jnp.zeros((r_pad, d), F32)], axis=0)
    mod_all = _ada_mod(c_all, ada_w, ada_b)
    mod_s = mod_all[:, :, None, :db, :]
    mod_p = mod_all[:, :, db : db + bp, None, :]

    tf = dff // 2 if (dff // 2) % 128 == 0 else dff

    def trunk(x, mod, tm, conv_prev, seq_mode, build_kv, attend):
        conv_states = []
        for l in range(depth):
            x = _ffn_block(x, mod, norm_g, ffn_w_in_b, ffn_w_out_b, l, 0, 0, tm, tf)
            if l < n_a:
                p0, p1 = conv_prev(l)
                x, st = _conv_block(x, mod, norm_g, conv_w_in_b, conv_k, conv_w_out_b, p0, p1, l, tm, seq_mode)
                conv_states.append(st)
            else:
                j = l - n_a
                x = attend(x, mod, l, j)
            x = _ffn_block(x, mod, norm_g, ffn_w_in_b, ffn_w_out_b, l, 2, 1, tm, tf)
            if l == n_a - 1:
                build_kv(x)
        return x, conv_states

    tm_p = _pick_tile(seq, 512)
    cos_p, sin_p = _rope_tables(jnp.arange(seq, dtype=jnp.int32))
    ng_p = seq // CMP_STRIDE
    nc_p = (seq - CMP_LEN) // CMP_STRIDE + 1
    ns_p = -(-seq // SLC_BLOCK)
    assert ns_p <= NS_PAD and seq % Q_BLOCK == 0 and seq >= WINDOW + Q_BLOCK
    ov_t_p = _overlap(ng_p, NS_PAD).T.astype(BF)
    tk_p = _pick_tile(seq, 512)
    ctx_p = {}

    def build_prompt(x):
        rows, win = _kv_proj(x, kv_norm_g, w_kv_b, cos_p, sin_p, tm_p)
        cmp = _compress_prompt(rows, pe2, w1bd, w2bd)

        def heads_major(a):
            return a.reshape(bp, a.shape[1], N_KV_HEADS, HEAD_DIM).transpose(0, 2, 1, 3).astype(BF)

        def lanes_major(a):
            return a.transpose(0, 2, 1).astype(BF)

        ctx_p.update(
            rows=rows, win=win,
            kc=heads_major(cmp[:, :, :KV_W]), vc_t=lanes_major(cmp[:, :, KV_W:]),
            ks=heads_major(rows[:, :, 2 * KV_W : 3 * KV_W]), vs_t=lanes_major(rows[:, :, 3 * KV_W :]),
            kw=heads_major(win[:, :, :KV_W]), vw_t=lanes_major(win[:, :, KV_W:]),
        )

    def attend_prompt(x, mod, l, j):
        qn, qr, gates = _qg_proj(x, mod, norm_g, wq_b[j], wg_b[j], cos_p, sin_p, l, tm_p)
        gates_t = jnp.pad(
            gates[:, :, : 3 * N_HEADS].reshape(bp, seq, N_KV_HEADS, 3 * GROUP), ((0, 0), (0, 0), (0, 0), (0, 16 - 3 * GROUP))
        ).transpose(0, 2, 3, 1)
        o_t = _attn_prompt(
            qn.transpose(0, 2, 1), qr.transpose(0, 2, 1), gates_t, ctx_p["kc"], ctx_p["vc_t"], ctx_p["ks"],
            ctx_p["vs_t"], ctx_p["kw"], ctx_p["vw_t"], ov_t_p, nc_p, ns_p, Q_BLOCK, tk_p,
        )
        return _out_proj(o_t.transpose(0, 2, 1), x, mod, norm_g, w_o_b[j], l, tm_p)

    zero_row = jnp.zeros((bp, 1, d), F32)
    y_prompt, conv_p = trunk(x_prompt, mod_p, tm_p, lambda l: (zero_row, zero_row), True, build_prompt, attend_prompt)
    kv_rows_p = ctx_p["rows"].reshape(bp, seq, 4, N_KV_HEADS, HEAD_DIM)
    wbp = min(WINDOW, seq)
    win_p = ctx_p["win"][:, seq - wbp :].reshape(bp, wbp, 2, N_KV_HEADS, HEAD_DIM)
    conv_p = jnp.stack(conv_p)

    xs = x_sample.reshape(1, db, d)
    tm_s = db
    cos_s, sin_s = _rope_tables(jnp.full((db,), past_len, jnp.int32))
    total = past_len + 1
    nc_s = (total - CMP_LEN) // CMP_STRIDE + 1
    ns_s = -(-total // SLC_BLOCK)
    ng_s = past_len // CMP_STRIDE
    assert ns_s <= NS_PAD and ng_s <= 128 and past_len // SLC_BLOCK < ns_s
    cache2 = cache_kv.reshape(cache_kv.shape[0], page, 4 * KV_W)
    win_state = state_win_kv.reshape(db, win_buf, 2 * KV_W)
    ov_s = _overlap(ng_s, NS_PAD).astype(BF)
    expand = (jnp.arange(past_len)[None, :] // SLC_BLOCK == jnp.arange(NS_PAD)[:, None]).astype(BF)
    head_kv = jnp.arange(N_HEADS) // GROUP
    gsum = (head_kv[:, None] == head_kv[None, :]).astype(BF)
    bd_mask = (head_kv[:, None] == jnp.arange(N_KV_HEADS)[None, :]).astype(BF)
    ctx_s = {}

    def build_sample(x):
        rows, win = _kv_proj(x, kv_norm_g, w_kv_b, cos_s, sin_s, tm_s)
        ctx_s.update(
            rows=rows.reshape(db, 1, 4 * KV_W), win=win.reshape(db, 1, 2 * KV_W),
            cmp=_compress_sample(cache2, page_table, pe2, w1bd, w2bd, total, 2 if db % 2 == 0 else 1),
        )

    def block_diag(q):
        qh = q.reshape(db, N_HEADS, 1, HEAD_DIM) * bd_mask[None, :, :, None]
        return qh.reshape(db, N_HEADS, KV_W)

    def attend_sample(x, mod, l, j):
        qn, qr, gates = _qg_proj(x, mod, norm_g, wq_b[j], wg_b[j], cos_s, sin_s, l, tm_s)
        o_full = _attn_sample(
            cache2, page_table, block_diag(qn[0]), block_diag(qr[0]), gates[0, :, : 3 * N_HEADS].reshape(db, N_HEADS, 3),
            ctx_s["cmp"], ctx_s["rows"], win_state, ctx_s["win"], ov_s, expand, gsum, nc_s, ns_s,
        )
        o = jnp.take_along_axis(
            o_full.reshape(db, N_HEADS, N_KV_HEADS, HEAD_DIM), head_kv[None, :, None, None], axis=2
        ).reshape(1, db, nq)
        return _out_proj(o.astype(BF), x, mod, norm_g, w_o_b[j], l, tm_s)

    def conv_prev_s(l):
        return state_conv[l, :, 0, :].reshape(1, db, d), state_conv[l, :, 1, :].reshape(1, db, d)

    y_s, conv_u = trunk(xs, mod_s, tm_s, conv_prev_s, False, build_sample, attend_sample)
    y_sample = y_s.reshape(db, 1, d)
    kv_rows_s = ctx_s["rows"].reshape(db, 1, 4, N_KV_HEADS, HEAD_DIM)
    win_s = jnp.concatenate([state_win_kv, ctx_s["win"].reshape(db, 1, 2, N_KV_HEADS, HEAD_DIM)], axis=1)[:, 1:]
    conv_s = jnp.stack([jnp.stack([state_conv[l, :, 1, :], conv_u[l][0]], axis=1) for l in range(n_a)])

    return (y_prompt, y_sample, kv_rows_p, win_p, conv_p, kv_rows_s, win_s, conv_s)
```

```python
import functools

import jax
import jax.numpy as jnp
from jax import lax
from jax.experimental import pallas as pl
from jax.experimental.pallas import tpu as pltpu

F32 = jnp.float32
BF = jnp.bfloat16

HEAD_DIM = 64
N_HEADS = 16
N_KV_HEADS = 4
GROUP = N_HEADS // N_KV_HEADS
KV_W = N_KV_HEADS * HEAD_DIM
CMP_LEN = 32
CMP_STRIDE = 16
SLC_BLOCK = 64
TOP_N = 16
WINDOW = 512
Q_BLOCK = 128
CONV_W = 3
MACARON_W = 0.5
ROPE_THETA = 10000.0
EPS = 1e-6
FORCE_BONUS = 1e3
LOG2E = 1.4426950408889634
NEG = -1e30
N_MOD = 9
NS_PAD = 128
VMEM_LIMIT = 52 * 1024 * 1024


def _cp(sem):
    return pltpu.CompilerParams(dimension_semantics=sem, vmem_limit_bytes=VMEM_LIMIT)


def _dot(a, b):
    return jnp.dot(a, b, preferred_element_type=F32)


def _dot_nt(a, b):
    return lax.dot_general(a, b, (((1,), (1,)), ((), ())), preferred_element_type=F32)


def _split_dot(a, b_f32):
    hi = b_f32.astype(BF)
    lo = (b_f32 - hi.astype(F32)).astype(BF)
    return _dot(a, hi) + _dot(a, lo)


def _rms(x, g):
    return x * lax.rsqrt(jnp.mean(x * x, axis=-1, keepdims=True) + EPS) * g


def _silu(a):
    return a * jax.nn.sigmoid(a)


def _rope(x, cos, sin_signed):
    n = x.shape[-1]
    lane = lax.broadcasted_iota(jnp.int32, x.shape, 1)
    first = (lane & (HEAD_DIM // 2)) == 0
    partner = jnp.where(first, pltpu.roll(x, n - HEAD_DIM // 2, axis=1), pltpu.roll(x, HEAD_DIM // 2, axis=1))
    return x * cos + partner * sin_signed


def _ada_kernel(c_ref, w_ref, b_ref, o_ref):
    cond = _silu(c_ref[...]).astype(BF)
    o_ref[0, 0] = _dot(cond, w_ref[0].astype(BF)) + b_ref[0, 0]


def _ada_mod(c_all, ada_w, ada_b):
    depth, d, _ = ada_w.shape
    r = c_all.shape[0]
    return pl.pallas_call(
        _ada_kernel,
        grid=(depth, N_MOD),
        in_specs=[
            pl.BlockSpec((r, d), lambda l, j: (0, 0)),
            pl.BlockSpec((1, d, d), lambda l, j: (l, 0, j)),
            pl.BlockSpec((1, 1, 1, d), lambda l, j: (l, j, 0, 0)),
        ],
        out_specs=pl.BlockSpec((1, 1, r, d), lambda l, j: (l, j, 0, 0)),
        out_shape=jax.ShapeDtypeStruct((depth, N_MOD, r, d), F32),
        compiler_params=_cp(("parallel", "parallel")),
        name="ada_mod",
    )(c_all, ada_w, ada_b.reshape(depth, N_MOD, 1, d))


def _row_specs(bx, tm, d, mod, layer, sub):
    lm = mod.shape[3]
    tml = 1 if lm == 1 else tm
    x_spec = pl.BlockSpec((1, tm, d), lambda b, i, *_: (b, i, 0))
    mod_spec = pl.BlockSpec(
        (None, 3, None, tml, d), lambda b, i, *_: (layer, sub, b, 0 if lm == 1 else i, 0)
    )
    g_spec = pl.BlockSpec((None, 6, d), lambda b, i, *_: (layer, 0, 0))
    return x_spec, mod_spec, g_spec


def _pre(x, g_ref, m, sub):
    return _rms(x, g_ref[2 * sub : 2 * sub + 1, :]) * (1.0 + m[1]) + m[0]


def _post(x, o, g_ref, m, sub, w):
    return x + w * m[2] * _rms(o, g_ref[2 * sub + 1 : 2 * sub + 2, :])


def _ffn_kernel(x_ref, mod_ref, g_ref, wa_ref, wb_ref, wo_ref, y_ref, h_scr, acc_scr, *, sub, nf):
    f = pl.program_id(2)

    @pl.when(f == 0)
    def _():
        h_scr[...] = _pre(x_ref[0], g_ref, mod_ref[...], sub).astype(BF)

    h = h_scr[...]
    u = _silu(_dot(h, wa_ref[...])) * _dot(h, wb_ref[...])
    part = _dot(u.astype(BF), wo_ref[...])

    @pl.when(f == 0)
    def _():
        acc_scr[...] = part

    @pl.when(f > 0)
    def _():
        acc_scr[...] += part

    @pl.when(f == nf - 1)
    def _():
        y_ref[0] = _post(x_ref[0], acc_scr[...], g_ref, mod_ref[...], sub, MACARON_W)


def _ffn_block(x, mod, norm_g, w_in, w_out, layer, sub, half, tm, tf):
    bx, L, d = x.shape
    dff = w_out.shape[2]
    nf = dff // tf
    x_spec, mod_spec, g_spec = _row_specs(bx, tm, d, mod, layer, sub)
    return pl.pallas_call(
        functools.partial(_ffn_kernel, sub=sub, nf=nf),
        grid=(bx, L // tm, nf),
        in_specs=[
            x_spec,
            mod_spec,
            g_spec,
            pl.BlockSpec((None, None, d, tf), lambda b, i, f: (layer, half, 0, f)),
            pl.BlockSpec((None, None, d, tf), lambda b, i, f: (layer, half, 0, nf + f)),
            pl.BlockSpec((None, None, tf, d), lambda b, i, f: (layer, half, f, 0)),
        ],
        out_specs=pl.BlockSpec((1, tm, d), lambda b, i, f: (b, i, 0)),
        out_shape=jax.ShapeDtypeStruct((bx, L, d), F32),
        scratch_shapes=[pltpu.VMEM((tm, d), BF), pltpu.VMEM((tm, d), F32)],
        compiler_params=_cp(("parallel", "parallel", "arbitrary")),
        name="ffn_block",
    )(x, mod, norm_g, w_in, w_in, w_out)


def _conv_kernel(x_ref, mod_ref, g_ref, win_ref, ck_ref, wout_ref, p0_ref, p1_ref, y_ref, u_ref, u_scr, *, seq_mode, tm):
    d = x_ref.shape[-1]
    x = x_ref[0]
    m = mod_ref[...]
    h = _pre(x, g_ref, m, 1).astype(BF)
    z = _dot(h, win_ref[...])
    bg = z[:, 0:d]
    u = z[:, d : 2 * d] * z[:, 2 * d : 3 * d]
    ck = ck_ref[...]
    if seq_mode:
        @pl.when(pl.program_id(1) == 0)
        def _():
            u_scr[6:7, :] = p0_ref[0]
            u_scr[7:8, :] = p1_ref[0]

        u_scr[8 : 8 + tm, :] = u
        prev2 = u_scr[6 : 6 + tm, :]
        prev1 = u_scr[7 : 7 + tm, :]
        u_scr[6:8, :] = u[tm - 2 : tm, :]
        u_ref[0] = u[tm - 2 : tm, :]
    else:
        prev2 = p0_ref[0]
        prev1 = p1_ref[0]
        u_ref[0] = u
    y = ck[0:1, :] * prev2 + ck[1:2, :] * prev1 + ck[2:3, :] * u
    o = _dot((bg * y).astype(BF), wout_ref[...])
    y_ref[0] = _post(x, o, g_ref, m, 1, 1.0)


def _conv_block(x, mod, norm_g, w_in, conv_k, w_out, prev0, prev1, layer, tm, seq_mode):
    bx, L, d = x.shape
    x_spec, mod_spec, g_spec = _row_specs(bx, tm, d, mod, layer, 1)
    if seq_mode:
        p_spec = pl.BlockSpec((1, 1, d), lambda b, i: (b, 0, 0))
        u_spec = pl.BlockSpec((1, 2, d), lambda b, i: (b, 0, 0))
        u_shape = jax.ShapeDtypeStruct((bx, 2, d), F32)
    else:
        p_spec = pl.BlockSpec((1, tm, d), lambda b, i: (b, i, 0))
        u_spec = pl.BlockSpec((1, tm, d), lambda b, i: (b, i, 0))
        u_shape = jax.ShapeDtypeStruct((bx, L, d), F32)
    return pl.pallas_call(
        functools.partial(_conv_kernel, seq_mode=seq_mode, tm=tm),
        grid=(bx, L // tm),
        in_specs=[
            x_spec,
            mod_spec,
            g_spec,
            pl.BlockSpec((None, d, 3 * d), lambda b, i: (layer, 0, 0)),
            pl.BlockSpec((None, CONV_W, d), lambda b, i: (layer, 0, 0)),
            pl.BlockSpec((None, d, d), lambda b, i: (layer, 0, 0)),
            p_spec,
            p_spec,
        ],
        out_specs=[pl.BlockSpec((1, tm, d), lambda b, i: (b, i, 0)), u_spec],
        out_shape=[jax.ShapeDtypeStruct((bx, L, d), F32), u_shape],
        scratch_shapes=[pltpu.VMEM((tm + 8, d), F32)],
        compiler_params=_cp(("parallel", "arbitrary")),
        name="conv_block",
    )(x, mod, norm_g, w_in, conv_k, w_out, prev0, prev1)


def _kv_kernel(x_ref, g_ref, w_ref, cos_ref, sin_ref, rows_ref, win_ref):
    hn = _rms(x_ref[0], g_ref[...]).astype(BF)
    kv = _dot(hn, w_ref[...])
    cos = cos_ref[...]
    sin = sin_ref[...]
    rows_ref[0, :, 0 : 2 * KV_W] = kv[:, 0 : 2 * KV_W]
    rows_ref[0, :, 2 * KV_W : 3 * KV_W] = _rope(kv[:, 2 * KV_W : 3 * KV_W], cos, sin)
    rows_ref[0, :, 3 * KV_W : 4 * KV_W] = kv[:, 3 * KV_W : 4 * KV_W]
    win_ref[0, :, 0:KV_W] = _rope(kv[:, 4 * KV_W : 5 * KV_W], cos, sin)
    win_ref[0, :, KV_W : 2 * KV_W] = kv[:, 5 * KV_W : 6 * KV_W]


def _kv_proj(x, kv_norm_g, w_kv, cos, sin, tm):
    bx, L, d = x.shape
    return pl.pallas_call(
        _kv_kernel,
        grid=(bx, L // tm),
        in_specs=[
            pl.BlockSpec((1, tm, d), lambda b, i: (b, i, 0)),
            pl.BlockSpec((1, d), lambda b, i: (0, 0)),
            pl.BlockSpec((d, 6 * KV_W), lambda b, i: (0, 0)),
            pl.BlockSpec((tm, KV_W), lambda b, i: (i, 0)),
            pl.BlockSpec((tm, KV_W), lambda b, i: (i, 0)),
        ],
        out_specs=[
            pl.BlockSpec((1, tm, 4 * KV_W), lambda b, i: (b, i, 0)),
            pl.BlockSpec((1, tm, 2 * KV_W), lambda b, i: (b, i, 0)),
        ],
        out_shape=[
            jax.ShapeDtypeStruct((bx, L, 4 * KV_W), F32),
            jax.ShapeDtypeStruct((bx, L, 2 * KV_W), F32),
        ],
        compiler_params=_cp(("parallel", "parallel")),
        name="kv_proj",
    )(x, kv_norm_g.reshape(1, d), w_kv, cos, sin)


def _qg_kernel(x_ref, mod_ref, g_ref, wq_ref, wg_ref, cos_ref, sin_ref, qn_ref, qr_ref, gate_ref):
    h = _pre(x_ref[0], g_ref, mod_ref[...], 1).astype(BF)
    q = _dot(h, wq_ref[...])
    gate_ref[0] = jax.nn.sigmoid(_dot(h, wg_ref[...]))
    scale = HEAD_DIM**-0.5 * LOG2E
    qn_ref[0] = (q * scale).astype(BF)
    cos = cos_ref[...]
    sin = sin_ref[...]
    for c in range(q.shape[1] // KV_W):
        sl = slice(c * KV_W, (c + 1) * KV_W)
        qr_ref[0, :, sl] = (_rope(q[:, sl], cos, sin) * scale).astype(BF)


def _qg_proj(x, mod, norm_g, wq, wg, cos, sin, layer, tm):
    bx, L, d = x.shape
    nq = wq.shape[-1]
    x_spec, mod_spec, g_spec = _row_specs(bx, tm, d, mod, layer, 1)
    return pl.pallas_call(
        _qg_kernel,
        grid=(bx, L // tm),
        in_specs=[
            x_spec,
            mod_spec,
            g_spec,
            pl.BlockSpec((d, nq), lambda b, i: (0, 0)),
            pl.BlockSpec((d, 128), lambda b, i: (0, 0)),
            pl.BlockSpec((tm, KV_W), lambda b, i: (i, 0)),
            pl.BlockSpec((tm, KV_W), lambda b, i: (i, 0)),
        ],
        out_specs=[
            pl.BlockSpec((1, tm, nq), lambda b, i: (b, i, 0)),
            pl.BlockSpec((1, tm, nq), lambda b, i: (b, i, 0)),
            pl.BlockSpec((1, tm, 128), lambda b, i: (b, i, 0)),
        ],
        out_shape=[
            jax.ShapeDtypeStruct((bx, L, nq), BF),
            jax.ShapeDtypeStruct((bx, L, nq), BF),
            jax.ShapeDtypeStruct((bx, L, 128), F32),
        ],
        compiler_params=_cp(("parallel", "parallel")),
        name="qg_proj",
    )(x, mod, norm_g, wq, wg, cos, sin)


def _op_kernel(o_ref, x_ref, mod_ref, g_ref, w_ref, y_ref):
    a = _dot(o_ref[0], w_ref[...])
    y_ref[0] = _post(x_ref[0], a, g_ref, mod_ref[...], 1, 1.0)


def _out_proj(o, x, mod, norm_g, w_o, layer, tm):
    bx, L, d = x.shape
    x_spec, mod_spec, g_spec = _row_specs(bx, tm, d, mod, layer, 1)
    return pl.pallas_call(
        _op_kernel,
        grid=(bx, L // tm),
        in_specs=[
            pl.BlockSpec((1, tm, o.shape[-1]), lambda b, i: (b, i, 0)),
            x_spec,
            mod_spec,
            g_spec,
            pl.BlockSpec(w_o.shape, lambda b, i: (0, 0)),
        ],
        out_specs=pl.BlockSpec((1, tm, d), lambda b, i: (b, i, 0)),
        out_shape=jax.ShapeDtypeStruct((bx, L, d), F32),
        compiler_params=_cp(("parallel", "parallel")),
        name="out_proj",
    )(o, x, mod, norm_g, w_o)


def _compress_tile(get_x, pe_ref, w1_ref, w2_ref, b_scr, m_rows):
    half = CMP_LEN // 2
    xs = [get_x(l) for l in range(half)]
    xa = jnp.concatenate([(xs[l] + pe_ref[l]).astype(BF) for l in range(half)], axis=1)
    xb = jnp.concatenate([(xs[l] + pe_ref[half + l]).astype(BF) for l in range(half)], axis=1)
    a = _dot(xa, w1_ref[0])
    b_scr[0:m_rows, :] = _dot(xb, w1_ref[1])
    b_scr[m_rows : m_rows + 8, :] = jnp.zeros((8, b_scr.shape[1]), F32)
    hid = a + b_scr[1 : m_rows + 1, :]
    return _dot(_silu(hid).astype(BF), w2_ref[...])


def _cmp_prompt_kernel(x_ref, pe_ref, w1_ref, w2_ref, o_ref, b_scr, *, ng, nc):
    row = lax.broadcasted_iota(jnp.int32, (ng, 128), 0)
    res = _compress_tile(
        lambda l: x_ref[0, pl.ds(l, ng, stride=CMP_STRIDE), :], pe_ref, w1_ref, w2_ref, b_scr, ng
    )
    o_ref[0] = jnp.where(row < nc, res, 0.0)


def _cmp_weights(cmp_pe, cmp_w1, cmp_w2):
    z1 = jnp.zeros_like(cmp_w1)
    w1bd = jnp.concatenate(
        [jnp.concatenate([cmp_w1, z1], axis=3), jnp.concatenate([z1, cmp_w1], axis=3)], axis=2
    )
    hid2 = w1bd.shape[-1]
    w1bd = w1bd.reshape(2, 2, (CMP_LEN // 2) * 2 * HEAD_DIM, hid2).astype(BF)
    z2 = jnp.zeros_like(cmp_w2)
    w2bd = jnp.concatenate(
        [jnp.concatenate([cmp_w2, z2], axis=2), jnp.concatenate([z2, cmp_w2], axis=2)], axis=1
    ).astype(BF)
    pe2 = jnp.concatenate([cmp_pe, cmp_pe], axis=2)[:, :, None, :]
    return pe2, w1bd, w2bd


def _compress_prompt(rows, pe2, w1bd, w2bd):
    bx, L, _ = rows.shape
    ng = L // CMP_STRIDE
    nc = (L - CMP_LEN) // CMP_STRIDE + 1
    kdim, hid2 = w1bd.shape[2], w1bd.shape[3]
    return pl.pallas_call(
        functools.partial(_cmp_prompt_kernel, ng=ng, nc=nc),
        grid=(4, bx),
        in_specs=[
            pl.BlockSpec((1, L, 128), lambda cg, b: (b, 0, cg)),
            pl.BlockSpec((None, CMP_LEN, 1, 128), lambda cg, b: (cg // 2, 0, 0, 0)),
            pl.BlockSpec((None, 2, kdim, hid2), lambda cg, b: (cg // 2, 0, 0, 0)),
            pl.BlockSpec((None, hid2, 128), lambda cg, b: (cg // 2, 0, 0)),
        ],
        out_specs=pl.BlockSpec((1, ng, 128), lambda cg, b: (b, 0, cg)),
        out_shape=jax.ShapeDtypeStruct((bx, ng, 2 * KV_W), F32),
        scratch_shapes=[pltpu.VMEM((ng + 8, hid2), F32)],
        compiler_params=_cp(("parallel", "parallel")),
        name="compress_prompt",
    )(rows, pe2, w1bd, w2bd)


def _cmp_sample_kernel(pt_ref, *refs, n_pages, sb, gpp, nc):
    del pt_ref
    page_refs = refs[: sb * n_pages]
    pe_ref, w1_ref, w2_ref, o_ref, b_scr = refs[sb * n_pages :]
    ng = n_pages * gpp
    m_rows = sb * ng
    row = lax.broadcasted_iota(jnp.int32, (ng, 128), 0)

    def get_x(l):
        return jnp.concatenate([r[0, pl.ds(l, gpp, stride=CMP_STRIDE), :] for r in page_refs], axis=0)

    res = _compress_tile(get_x, pe_ref, w1_ref, w2_ref, b_scr, m_rows)
    for s in range(sb):
        o_ref[s] = jnp.where(row < nc, res[s * ng : (s + 1) * ng, :], 0.0)


def _compress_sample(cache2, page_table, pe2, w1bd, w2bd, total_len, sb):
    db, n_pages = page_table.shape
    page = cache2.shape[1]
    gpp = page // CMP_STRIDE
    ng = n_pages * gpp
    nc = (total_len - CMP_LEN) // CMP_STRIDE + 1
    assert nc <= ng
    kdim, hid2 = w1bd.shape[2], w1bd.shape[3]

    def page_spec(s, p):
        return pl.BlockSpec((1, page, 128), lambda cg, bb, pt: (pt[bb * sb + s, p], 0, cg))

    grid_spec = pltpu.PrefetchScalarGridSpec(
        num_scalar_prefetch=1,
        grid=(4, db // sb),
        in_specs=[page_spec(s, p) for s in range(sb) for p in range(n_pages)]
        + [
            pl.BlockSpec((None, CMP_LEN, 1, 128), lambda cg, bb, pt: (cg // 2, 0, 0, 0)),
            pl.BlockSpec((None, 2, kdim, hid2), lambda cg, bb, pt: (cg // 2, 0, 0, 0)),
            pl.BlockSpec((None, hid2, 128), lambda cg, bb, pt: (cg // 2, 0, 0)),
        ],
        out_specs=pl.BlockSpec((sb, ng, 128), lambda cg, bb, pt: (bb, 0, cg)),
        scratch_shapes=[pltpu.VMEM((sb * ng + 8, hid2), F32)],
    )
    return pl.pallas_call(
        functools.partial(_cmp_sample_kernel, n_pages=n_pages, sb=sb, gpp=gpp, nc=nc),
        grid_spec=grid_spec,
        out_shape=jax.ShapeDtypeStruct((db, ng, 2 * KV_W), F32),
        compiler_params=_cp(("parallel", "parallel")),
        name="compress_sample",
    )(page_table, *([cache2] * (sb * n_pages)), pe2, w1bd, w2bd)


def _select_scores(imp, blk, pos):
    cur = pos // SLC_BLOCK
    valid = blk * SLC_BLOCK <= pos
    forced = (blk == 0) | (blk == cur) | (blk == cur - 1)
    return jnp.where(valid, imp + FORCE_BONUS * forced.astype(F32), -1.0)


def _topk_rows(score, k):
    n = score.shape[0]
    ridx = lax.broadcasted_iota(jnp.int32, score.shape, 0).astype(F32)

    def body(_, sc):
        m = jnp.max(sc, axis=0, keepdims=True)
        first = jnp.min(jnp.where(sc == m, ridx, float(n)), axis=0, keepdims=True)
        return jnp.where(ridx == first, -jnp.inf, sc)

    sc = lax.fori_loop(0, k, body, score, unroll=True)
    return (sc == -jnp.inf) & (score >= 0.0)


BIAS_ROWS = 16


def _masked_attend(k_rows, q, bias, v_aug):
    s = _dot(k_rows, q) + jnp.concatenate([bias] * GROUP, axis=1)
    e = jnp.exp2(s - jnp.max(s, axis=0, keepdims=True))
    return e, _dot(v_aug, e.astype(BF))


def _attn_prompt_kernel(
    qn_ref, qr_ref, g_ref, kc_ref, vct_ref, ks_ref, vst_ref, kw_ref, vwt_ref, ovt_ref, ctab_ref, wtab_ref,
    o_ref, bias_scr, qaug_scr, sa_scr, sb_scr, m_scr, acc_scr, *, tq, tk, ns, top_n,
):
    start = pl.program_id(2) * tq
    gq = GROUP * tq
    bpc = tk // SLC_BLOCK

    def heads_on_lanes(ref):
        t = ref[0]
        return jnp.concatenate([t[HEAD_DIM * g : HEAD_DIM * (g + 1), :] for g in range(GROUP)], axis=1)

    qn = heads_on_lanes(qn_ref)
    qr = heads_on_lanes(qr_ref)
    pos = start + lax.broadcasted_iota(jnp.int32, (1, tq), 1)
    pos_g = jnp.concatenate([pos] * GROUP, axis=1)

    ngp = kc_ref.shape[2]
    c0 = pl.multiple_of(ngp - start // CMP_STRIDE, 8)
    e_c, un_c = _masked_attend(kc_ref[0, 0], qn, ctab_ref[pl.ds(c0, ngp), :], vct_ref[0, 0])
    inv_c = jnp.where(pos_g >= CMP_LEN - 1, 1.0 / jnp.maximum(un_c[HEAD_DIM : HEAD_DIM + 1, :], 1e-30), 0.0)
    o_c = un_c[0:HEAD_DIM, :] * inv_c
    psum = e_c[:, 0:tq] * inv_c[:, 0:tq]
    for g in range(1, GROUP):
        psum = psum + e_c[:, g * tq : (g + 1) * tq] * inv_c[:, g * tq : (g + 1) * tq]
    imp = _split_dot(ovt_ref[...], psum)

    blk = lax.broadcasted_iota(jnp.int32, (NS_PAD, 1), 0)
    score = jnp.where(blk < ns, _select_scores(imp, blk, pos), -jnp.inf)
    sel_bias = jnp.where(_topk_rows(score, top_n), 0.0, NEG)
    for c in range(NS_PAD // bpc):
        bias_scr[BIAS_ROWS * c : BIAS_ROWS * c + bpc, :] = sel_bias[bpc * c : bpc * (c + 1), :]
        if bpc < BIAS_ROWS:
            bias_scr[BIAS_ROWS * c + bpc : BIAS_ROWS * (c + 1), :] = jnp.zeros((BIAS_ROWS - bpc, tq), F32)

    wk = WINDOW + tq
    w0 = pl.multiple_of(jnp.maximum(start - WINDOW, 0), 128)
    t0 = pl.multiple_of(WINDOW - (start - w0), 128)
    _, un_w = _masked_attend(
        kw_ref[0, 0, pl.ds(w0, wk), :], qr, wtab_ref[pl.ds(t0, wk), :], vwt_ref[0, 0, :, pl.ds(w0, wk)]
    )
    o_w = un_w[0:HEAD_DIM, :] * (1.0 / jnp.maximum(un_w[HEAD_DIM : HEAD_DIM + 1, :], 1e-30))

    qaug_scr[0:HEAD_DIM, :] = qr
    qaug_scr[HEAD_DIM:, :] = jnp.zeros((qaug_scr.shape[0] - HEAD_DIM, gq), BF)
    m_scr[...] = jnp.full(m_scr.shape, NEG, F32)
    acc_scr[...] = jnp.zeros(acc_scr.shape, F32)

    def scores(c, s_ref):
        k0 = pl.multiple_of(c * tk, tk)
        b = bias_scr[pl.ds(pl.multiple_of(c * BIAS_ROWS, BIAS_ROWS), BIAS_ROWS), :]
        qaug_scr[HEAD_DIM : HEAD_DIM + BIAS_ROWS, :] = jnp.concatenate([b] * GROUP, axis=1).astype(BF)
        s_ref[...] = _dot(ks_ref[0, 0, pl.ds(k0, tk), :], qaug_scr[...])

    def accumulate(c, s_ref, causal):
        k0 = pl.multiple_of(c * tk, tk)
        s = s_ref[...]
        if causal:
            kpos = k0 + lax.broadcasted_iota(jnp.int32, (tk, 1), 0)
            s = s + jnp.concatenate([jnp.where(kpos <= pos, 0.0, NEG)] * GROUP, axis=1)
        m_old = m_scr[...]
        m_new = jnp.maximum(m_old, jnp.max(s, axis=0, keepdims=True))
        p = jnp.exp2(s - m_new).astype(BF)
        acc_scr[...] = jnp.exp2(m_old - m_new) * acc_scr[...] + _dot(vst_ref[0, 0, :, pl.ds(k0, tk)], p)
        m_scr[...] = m_new

    def pair(j, carry):
        scores(2 * j + 1, sb_scr)
        accumulate(2 * j, sa_scr, False)
        scores(2 * j + 2, sa_scr)
        accumulate(2 * j + 1, sb_scr, False)
        return carry

    last = (start + tq - 1) // tk
    scores(0, sa_scr)
    lax.fori_loop(0, last // 2, pair, 0)

    @pl.when(last % 2 == 0)
    def _():
        accumulate(last, sa_scr, True)

    @pl.when(last % 2 == 1)
    def _():
        scores(last, sb_scr)
        accumulate(last - 1, sa_scr, False)
        accumulate(last, sb_scr, True)

    acc = acc_scr[...]
    o_s = acc[0:HEAD_DIM, :] * (1.0 / jnp.maximum(acc[HEAD_DIM : HEAD_DIM + 1, :], 1e-30))

    gates = g_ref[0, 0]
    outs = []
    for g in range(GROUP):
        sl = slice(g * tq, (g + 1) * tq)
        outs.append(
            gates[3 * g : 3 * g + 1, :] * o_c[:, sl]
            + gates[3 * g + 1 : 3 * g + 2, :] * o_s[:, sl]
            + gates[3 * g + 2 : 3 * g + 3, :] * o_w[:, sl]
        )
    o_ref[0] = jnp.concatenate(outs, axis=0).astype(BF)


def _attn_prompt(qn_t, qr_t, gates_t, kc, vc_t, ks, vs_t, kw, vw_t, ov_t, ctab, wtab, ns, tq, tk):
    bx, _, L = qn_t.shape
    ngp = kc.shape[2]
    gd = GROUP * HEAD_DIM
    kaug = ks.shape[-1]
    vaug = vs_t.shape[2]
    kv_spec = pl.BlockSpec((1, 1, L, HEAD_DIM), lambda b, k, i: (b, k, 0, 0))
    vt_spec = pl.BlockSpec((1, 1, vaug, L), lambda b, k, i: (b, k, 0, 0))
    q_spec = pl.BlockSpec((1, gd, tq), lambda b, k, i: (b, k, i))
    return pl.pallas_call(
        functools.partial(_attn_prompt_kernel, tq=tq, tk=tk, ns=ns, top_n=min(TOP_N, ns)),
        grid=(bx, N_KV_HEADS, L // tq),
        in_specs=[
            q_spec,
            q_spec,
            pl.BlockSpec((1, 1, 16, tq), lambda b, k, i: (b, k, 0, i)),
            pl.BlockSpec((1, 1, ngp, HEAD_DIM), lambda b, k, i: (b, k, 0, 0)),
            pl.BlockSpec((1, 1, vaug, ngp), lambda b, k, i: (b, k, 0, 0)),
            pl.BlockSpec((1, 1, L, kaug), lambda b, k, i: (b, k, 0, 0)),
            vt_spec,
            kv_spec,
            vt_spec,
            pl.BlockSpec(ov_t.shape, lambda b, k, i: (0, 0)),
            pl.BlockSpec(ctab.shape, lambda b, k, i: (0, 0)),
            pl.BlockSpec(wtab.shape, lambda b, k, i: (0, 0)),
        ],
        out_specs=q_spec,
        out_shape=jax.ShapeDtypeStruct(qn_t.shape, BF),
        scratch_shapes=[
            pltpu.VMEM((NS_PAD // (tk // SLC_BLOCK) * BIAS_ROWS, tq), F32),
            pltpu.VMEM((kaug, GROUP * tq), BF),
            pltpu.VMEM((tk, GROUP * tq), F32),
            pltpu.VMEM((tk, GROUP * tq), F32),
            pltpu.VMEM((1, GROUP * tq), F32),
            pltpu.VMEM((vaug, GROUP * tq), F32),
        ],
        compiler_params=_cp(("parallel", "parallel", "arbitrary")),
        name="attn_prompt",
    )(qn_t, qr_t, gates_t, kc, vc_t, ks, vs_t, kw, vw_t, ov_t, ctab, wtab)


def _softmax_rows_ext(s, mask, s_new, m_new_mask):
    sm = jnp.where(mask, s, NEG)
    sn = jnp.where(m_new_mask, s_new, NEG)
    mx = jnp.maximum(jnp.max(sm, axis=1, keepdims=True), sn)
    e = jnp.exp2(sm - mx) * mask.astype(F32)
    en = jnp.exp2(sn - mx) * m_new_mask.astype(F32)
    inv = 1.0 / jnp.maximum(jnp.sum(e, axis=1, keepdims=True) + en, 1e-30)
    return e * inv, en * inv


def _attn_sample_kernel(
    pt_ref, *refs, n_pages, page, past_len, nc, ns, top_n, win_buf,
):
    del pt_ref
    page_refs = refs[:n_pages]
    (qn_ref, qr_ref, g_ref, cmp_ref, new_ref, win_ref, wnew_ref, ov_ref, exp_ref, gsum_ref, o_ref) = refs[n_pages:]
    pos = past_len
    qn = qn_ref[0]
    qr = qr_ref[0]
    qr32 = qr.astype(F32)
    nh = qn.shape[0]

    cmp = cmp_ref[0]
    kc = cmp[:, 0:KV_W].astype(BF)
    vc = cmp[:, KV_W : 2 * KV_W].astype(BF)
    ngp = kc.shape[0]
    n_idx = lax.broadcasted_iota(jnp.int32, (1, ngp), 1)
    cmask = jnp.broadcast_to((n_idx * CMP_STRIDE + (CMP_LEN - 1) <= pos) & (n_idx < nc), (nh, ngp))
    sm = jnp.where(cmask, _dot_nt(qn, kc), NEG)
    e = jnp.exp2(sm - jnp.max(sm, axis=1, keepdims=True)) * cmask.astype(F32)
    p_c = e / jnp.maximum(jnp.sum(e, axis=1, keepdims=True), 1e-30)
    o_c = _dot(p_c.astype(BF), vc)

    t = _split_dot_r(p_c, ov_ref[...])
    imp = _split_dot(gsum_ref[...], t)
    blk = lax.broadcasted_iota(jnp.int32, (1, NS_PAD), 1)
    score = jnp.where(blk < ns, _select_scores(imp, blk, pos), -jnp.inf)
    pad = jnp.concatenate([score, jnp.zeros((NS_PAD - nh, NS_PAD), F32)], axis=0)
    score_t = pad.T
    ii = lax.broadcasted_iota(jnp.int32, (NS_PAD, NS_PAD), 0)
    jj = lax.broadcasted_iota(jnp.int32, (NS_PAD, NS_PAD), 1)
    head_row = lax.broadcasted_iota(jnp.int32, (nh, NS_PAD), 0)
    sel = jnp.zeros((nh, NS_PAD), F32)
    for h in range(nh):
        a = score_t[:, h : h + 1]
        b = score[h : h + 1, :]
        beats = (a > b) | ((a == b) & (ii < jj))
        rank = jnp.sum(beats.astype(F32), axis=0, keepdims=True)
        sel_h = jnp.where((rank < float(top_n)) & (b >= 0.0), 1.0, 0.0)
        sel = jnp.where(head_row == h, sel_h, sel)

    kmask = _dot(sel.astype(BF), exp_ref[...]) > 0.5
    new = new_ref[0]
    k_new = new[:, 2 * KV_W : 3 * KV_W]
    v_new = new[:, 3 * KV_W : 4 * KV_W]
    s_new = jnp.sum(qr32 * k_new.astype(BF).astype(F32), axis=1, keepdims=True)
    nb = pos // SLC_BLOCK
    new_sel = sel[:, nb : nb + 1] > 0.5
    s = jnp.concatenate([_dot_nt(qr, r[0][:, 0:KV_W].astype(BF)) for r in page_refs], axis=1)
    p_s, p_new = _softmax_rows_ext(s, kmask, s_new, new_sel)
    o_s = p_new * v_new.astype(BF).astype(F32)
    for i, r in enumerate(page_refs):
        o_s = o_s + _dot(p_s[:, i * page : (i + 1) * page].astype(BF), r[0][:, KV_W : 2 * KV_W].astype(BF))

    wbuf = win_ref[0]
    wnew = wnew_ref[0]
    kw_pos = (past_len - win_buf) + lax.broadcasted_iota(jnp.int32, (1, win_buf), 1)
    rel = pos - kw_pos
    wmask = jnp.broadcast_to((rel >= 0) & (rel < WINDOW) & (kw_pos >= 0), (nh, win_buf))
    s_w = _dot_nt(qr, wbuf[:, 0:KV_W].astype(BF))
    s_wn = jnp.sum(qr32 * wnew[:, 0:KV_W].astype(BF).astype(F32), axis=1, keepdims=True)
    p_w, p_wn = _softmax_rows_ext(s_w, wmask, s_wn, jnp.full((nh, 1), True))
    o_w = _dot(p_w.astype(BF), wbuf[:, KV_W : 2 * KV_W].astype(BF)) + p_wn * wnew[:, KV_W : 2 * KV_W].astype(BF).astype(F32)

    gates = g_ref[0]
    o_ref[0] = gates[:, 0:1] * o_c + gates[:, 1:2] * o_s + gates[:, 2:3] * o_w


def _split_dot_r(a_f32, b):
    hi = a_f32.astype(BF)
    lo = (a_f32 - hi.astype(F32)).astype(BF)
    return _dot(hi, b) + _dot(lo, b)


def _attn_sample(cache2, page_table, qn_bd, qr_bd, gates, cmp_s, rows_new, win_state, win_new, ov, expand, gsum, nc, ns):
    db, n_pages = page_table.shape
    page = cache2.shape[1]
    past_len = n_pages * page
    win_buf = win_state.shape[1]
    ngp = cmp_s.shape[1]

    def one(shape):
        return pl.BlockSpec((1,) + shape, lambda b, pt: (b, 0, 0))

    def const(a):
        return pl.BlockSpec(a.shape, lambda b, pt: (0, 0))

    grid_spec = pltpu.PrefetchScalarGridSpec(
        num_scalar_prefetch=1,
        grid=(db,),
        in_specs=[pl.BlockSpec((1, page, 2 * KV_W), (lambda b, pt, p=p: (pt[b, p], 0, 1))) for p in range(n_pages)]
        + [
            one((N_HEADS, KV_W)),
            one((N_HEADS, KV_W)),
            one((N_HEADS, 3)),
            one((ngp, 2 * KV_W)),
            one((1, 4 * KV_W)),
            one((win_buf, 2 * KV_W)),
            one((1, 2 * KV_W)),
            const(ov),
            const(expand),
            const(gsum),
        ],
        out_specs=one((N_HEADS, KV_W)),
    )
    return pl.pallas_call(
        functools.partial(
            _attn_sample_kernel, n_pages=n_pages, page=page, past_len=past_len, nc=nc, ns=ns,
            top_n=min(TOP_N, ns), win_buf=win_buf,
        ),
        grid_spec=grid_spec,
        out_shape=jax.ShapeDtypeStruct((db, N_HEADS, KV_W), F32),
        compiler_params=_cp(("parallel",)),
        name="attn_sample",
    )(page_table, *([cache2] * n_pages), qn_bd, qr_bd, gates, cmp_s, rows_new, win_state, win_new, ov, expand, gsum)


def _rope_tables(pos):
    half = HEAD_DIM // 2
    inv = ROPE_THETA ** (-jnp.arange(half, dtype=F32) / half)
    ang = pos.astype(F32)[:, None] * inv[None]
    cos, sin = jnp.cos(ang), jnp.sin(ang)
    cos = jnp.tile(jnp.concatenate([cos, cos], axis=1), (1, N_KV_HEADS))
    sin = jnp.tile(jnp.concatenate([-sin, sin], axis=1), (1, N_KV_HEADS))
    return cos, sin


def _overlap(n_rows, n_cols):
    cs = jnp.arange(n_rows)[:, None] * CMP_STRIDE
    ss = jnp.arange(n_cols)[None] * SLC_BLOCK
    return ((cs < ss + SLC_BLOCK) & (cs + CMP_LEN > ss)).astype(F32)


def _pick_tile(n, pref):
    t = min(n, pref)
    while n % t:
        t //= 2
    return t


def kernel(x_prompt, x_sample, cache_kv, state_win_kv, state_conv, page_table, c_prompt, c_sample,
           ada_w, ada_b, norm_g, ffn_w_in, ffn_w_out, conv_w_in, conv_k, conv_w_out,
           kv_norm_g, w_kv, cmp_pe, cmp_w1, cmp_w2, nsa_w_qg, nsa_w_o):
    bp, seq, d = x_prompt.shape
    db, dec_seq, _ = x_sample.shape
    assert dec_seq == 1
    depth = ada_w.shape[0]
    n_a = conv_w_in.shape[0]
    n_pages = page_table.shape[1]
    page = cache_kv.shape[1]
    past_len = n_pages * page
    win_buf = state_win_kv.shape[1]
    dff = ffn_w_out.shape[2]
    nq = N_HEADS * HEAD_DIM

    ffn_w_in_b = ffn_w_in.astype(BF)
    ffn_w_out_b = ffn_w_out.astype(BF)
    conv_w_in_b = conv_w_in.astype(BF)
    conv_w_out_b = conv_w_out.astype(BF)
    w_kv_b = w_kv.astype(BF)
    wq_b = nsa_w_qg[:, :, :nq].astype(BF)
    wg_b = jnp.pad(nsa_w_qg[:, :, nq:], ((0, 0), (0, 0), (0, 128 - 3 * N_HEADS))).astype(BF)
    w_o_b = nsa_w_o.astype(BF)
    pe2, w1bd, w2bd = _cmp_weights(cmp_pe, cmp_w1, cmp_w2)

    r_pad = -(db + bp) % 8
    c_all = jnp.concatenate([c_sample, c_prompt, jnp.zeros((r_pad, d), F32)], axis=0)
    mod_all = _ada_mod(c_all, ada_w, ada_b)
    mod_s = mod_all[:, :, None, :db, :]
    mod_p = mod_all[:, :, db : db + bp, None, :]

    tf = dff // 2 if (dff // 2) % 128 == 0 else dff

    def trunk(x, mod, tm, conv_prev, seq_mode, build_kv, attend):
        conv_states = []
        for l in range(depth):
            x = _ffn_block(x, mod, norm_g, ffn_w_in_b, ffn_w_out_b, l, 0, 0, tm, tf)
            if l < n_a:
                p0, p1 = conv_prev(l)
                x, st = _conv_block(x, mod, norm_g, conv_w_in_b, conv_k, conv_w_out_b, p0, p1, l, tm, seq_mode)
                conv_states.append(st)
            else:
                j = l - n_a
                x = attend(x, mod, l, j)
            x = _ffn_block(x, mod, norm_g, ffn_w_in_b, ffn_w_out_b, l, 2, 1, tm, tf)
            if l == n_a - 1:
                build_kv(x)
        return x, conv_states

    tm_p = _pick_tile(seq, 512)
    cos_p, sin_p = _rope_tables(jnp.arange(seq, dtype=jnp.int32))
    ng_p = seq // CMP_STRIDE
    ns_p = -(-seq // SLC_BLOCK)
    assert ns_p <= NS_PAD and seq % Q_BLOCK == 0 and seq >= WINDOW + Q_BLOCK
    ov_t_p = _overlap(ng_p, NS_PAD).T.astype(BF)
    tk_p = _pick_tile(seq, BIAS_ROWS * SLC_BLOCK)
    qoff = jnp.arange(Q_BLOCK)[None, :]
    cu = jnp.arange(2 * ng_p)[:, None] - ng_p
    ctab_p = jnp.where(cu * CMP_STRIDE + (CMP_LEN - 1) <= qoff, 0.0, NEG).astype(F32)
    wu = jnp.arange(2 * WINDOW + Q_BLOCK)[:, None]
    wtab_p = jnp.where((wu > qoff) & (wu <= qoff + WINDOW), 0.0, NEG).astype(F32)
    ctx_p = {}

    def build_prompt(x):
        rows, win = _kv_proj(x, kv_norm_g, w_kv_b, cos_p, sin_p, tm_p)
        cmp = _compress_prompt(rows, pe2, w1bd, w2bd)

        def heads_major(a):
            return a.reshape(bp, a.shape[1], N_KV_HEADS, HEAD_DIM).transpose(0, 2, 1, 3).astype(BF)

        def values_t(a):
            n = a.shape[1]
            ones_rows = jnp.broadcast_to((jnp.arange(BIAS_ROWS)[:, None] == 0).astype(BF), (BIAS_ROWS, n))
            return jnp.concatenate(
                [
                    a.transpose(0, 2, 1).astype(BF).reshape(bp, N_KV_HEADS, HEAD_DIM, n),
                    jnp.broadcast_to(ones_rows, (bp, N_KV_HEADS, BIAS_ROWS, n)),
                ],
                axis=2,
            )

        blk_in_chunk = (jnp.arange(seq) % tk_p) // SLC_BLOCK
        onehot = (blk_in_chunk[:, None] == jnp.arange(128 - HEAD_DIM)[None, :]).astype(BF)
        ks = jnp.concatenate(
            [heads_major(rows[:, :, 2 * KV_W : 3 * KV_W]), jnp.broadcast_to(onehot, (bp, N_KV_HEADS) + onehot.shape)],
            axis=3,
        )
        ctx_p.update(
            rows=rows, win=win,
            kc=heads_major(cmp[:, :, :KV_W]), vc_t=values_t(cmp[:, :, KV_W:]),
            ks=ks, vs_t=values_t(rows[:, :, 3 * KV_W :]),
            kw=heads_major(win[:, :, :KV_W]), vw_t=values_t(win[:, :, KV_W:]),
        )

    def attend_prompt(x, mod, l, j):
        qn, qr, gates = _qg_proj(x, mod, norm_g, wq_b[j], wg_b[j], cos_p, sin_p, l, tm_p)
        gates_t = jnp.pad(
            gates[:, :, : 3 * N_HEADS].reshape(bp, seq, N_KV_HEADS, 3 * GROUP), ((0, 0), (0, 0), (0, 0), (0, 16 - 3 * GROUP))
        ).transpose(0, 2, 3, 1)
        o_t = _attn_prompt(
            qn.transpose(0, 2, 1), qr.transpose(0, 2, 1), gates_t, ctx_p["kc"], ctx_p["vc_t"], ctx_p["ks"],
            ctx_p["vs_t"], ctx_p["kw"], ctx_p["vw_t"], ov_t_p, ctab_p, wtab_p, ns_p, Q_BLOCK, tk_p,
        )
        return _out_proj(o_t.transpose(0, 2, 1), x, mod, norm_g, w_o_b[j], l, tm_p)

    zero_row = jnp.zeros((bp, 1, d), F32)
    y_prompt, conv_p = trunk(x_prompt, mod_p, tm_p, lambda l: (zero_row, zero_row), True, build_prompt, attend_prompt)
    kv_rows_p = ctx_p["rows"].reshape(bp, seq, 4, N_KV_HEADS, HEAD_DIM)
    wbp = min(WINDOW, seq)
    win_p = ctx_p["win"][:, seq - wbp :].reshape(bp, wbp, 2, N_KV_HEADS, HEAD_DIM)
    conv_p = jnp.stack(conv_p)

    xs = x_sample.reshape(1, db, d)
    tm_s = db
    cos_s, sin_s = _rope_tables(jnp.full((db,), past_len, jnp.int32))
    total = past_len + 1
    nc_s = (total - CMP_LEN) // CMP_STRIDE + 1
    ns_s = -(-total // SLC_BLOCK)
    ng_s = past_len // CMP_STRIDE
    assert ns_s <= NS_PAD and ng_s <= 128 and past_len // SLC_BLOCK < ns_s
    cache2 = cache_kv.reshape(cache_kv.shape[0], page, 4 * KV_W)
    win_state = state_win_kv.reshape(db, win_buf, 2 * KV_W)
    ov_s = _overlap(ng_s, NS_PAD).astype(BF)
    expand = (jnp.arange(past_len)[None, :] // SLC_BLOCK == jnp.arange(NS_PAD)[:, None]).astype(BF)
    head_kv = jnp.arange(N_HEADS) // GROUP
    gsum = (head_kv[:, None] == head_kv[None, :]).astype(BF)
    bd_mask = (head_kv[:, None] == jnp.arange(N_KV_HEADS)[None, :]).astype(BF)
    ctx_s = {}

    def build_sample(x):
        rows, win = _kv_proj(x, kv_norm_g, w_kv_b, cos_s, sin_s, tm_s)
        ctx_s.update(
            rows=rows.reshape(db, 1, 4 * KV_W), win=win.reshape(db, 1, 2 * KV_W),
            cmp=_compress_sample(cache2, page_table, pe2, w1bd, w2bd, total, 2 if db % 2 == 0 else 1),
        )

    def block_diag(q):
        qh = q.reshape(db, N_HEADS, 1, HEAD_DIM) * bd_mask[None, :, :, None]
        return qh.reshape(db, N_HEADS, KV_W)

    def attend_sample(x, mod, l, j):
        qn, qr, gates = _qg_proj(x, mod, norm_g, wq_b[j], wg_b[j], cos_s, sin_s, l, tm_s)
        o_full = _attn_sample(
            cache2, page_table, block_diag(qn[0]), block_diag(qr[0]), gates[0, :, : 3 * N_HEADS].reshape(db, N_HEADS, 3),
            ctx_s["cmp"], ctx_s["rows"], win_state, ctx_s["win"], ov_s, expand, gsum, nc_s, ns_s,
        )
        o = jnp.take_along_axis(
            o_full.reshape(db, N_HEADS, N_KV_HEADS, HEAD_DIM), head_kv[None, :, None, None], axis=2
        ).reshape(1, db, nq)
        return _out_proj(o.astype(BF), x, mod, norm_g, w_o_b[j], l, tm_s)

    def conv_prev_s(l):
        return state_conv[l, :, 0, :].reshape(1, db, d), state_conv[l, :, 1, :].reshape(1, db, d)

    y_s, conv_u = trunk(xs, mod_s, tm_s, conv_prev_s, False, build_sample, attend_sample)
    y_sample = y_s.reshape(db, 1, d)
    kv_rows_s = ctx_s["rows"].reshape(db, 1, 4, N_KV_HEADS, HEAD_DIM)
    win_s = jnp.concatenate([state_win_kv, ctx_s["win"].reshape(db, 1, 2, N_KV_HEADS, HEAD_DIM)], axis=1)[:, 1:]
    conv_s = jnp.stack([jnp.stack([state_conv[l, :, 1, :], conv_u[l][0]], axis=1) for l in range(n_a)])

    return (y_prompt, y_sample, kv_rows_p, win_p, conv_p, kv_rows_s, win_s, conv_s)
```

```python
import functools

import jax
import jax.numpy as jnp
from jax import lax
from jax.experimental import pallas as pl
from jax.experimental.pallas import tpu as pltpu

F32 = jnp.float32
BF = jnp.bfloat16

HEAD_DIM = 64
N_HEADS = 16
N_KV_HEADS = 4
GROUP = N_HEADS // N_KV_HEADS
KV_W = N_KV_HEADS * HEAD_DIM
CMP_LEN = 32
CMP_STRIDE = 16
SLC_BLOCK = 64
TOP_N = 16
WINDOW = 512
Q_BLOCK = 128
CONV_W = 3
MACARON_W = 0.5
ROPE_THETA = 10000.0
EPS = 1e-6
FORCE_BONUS = 1e3
LOG2E = 1.4426950408889634
NEG = -1e30
N_MOD = 9
NS_PAD = 128
VMEM_LIMIT = 52 * 1024 * 1024


def _cp(sem):
    return pltpu.CompilerParams(dimension_semantics=sem, vmem_limit_bytes=VMEM_LIMIT)


def _dot(a, b):
    return jnp.dot(a, b, preferred_element_type=F32)


def _dot_nt(a, b):
    return lax.dot_general(a, b, (((1,), (1,)), ((), ())), preferred_element_type=F32)


def _split_dot(a, b_f32):
    hi = b_f32.astype(BF)
    lo = (b_f32 - hi.astype(F32)).astype(BF)
    return _dot(a, hi) + _dot(a, lo)


def _rms(x, g):
    return x * lax.rsqrt(jnp.mean(x * x, axis=-1, keepdims=True) + EPS) * g


def _silu(a):
    return a * jax.nn.sigmoid(a)


def _rope(x, cos, sin_signed):
    n = x.shape[-1]
    lane = lax.broadcasted_iota(jnp.int32, x.shape, 1)
    first = (lane & (HEAD_DIM // 2)) == 0
    partner = jnp.where(first, pltpu.roll(x, n - HEAD_DIM // 2, axis=1), pltpu.roll(x, HEAD_DIM // 2, axis=1))
    return x * cos + partner * sin_signed


def _ada_kernel(c_ref, w_ref, b_ref, o_ref):
    cond = _silu(c_ref[...]).astype(BF)
    o_ref[0, 0] = _dot(cond, w_ref[0].astype(BF)) + b_ref[0, 0]


def _ada_mod(c_all, ada_w, ada_b):
    depth, d, _ = ada_w.shape
    r = c_all.shape[0]
    return pl.pallas_call(
        _ada_kernel,
        grid=(depth, N_MOD),
        in_specs=[
            pl.BlockSpec((r, d), lambda l, j: (0, 0)),
            pl.BlockSpec((1, d, d), lambda l, j: (l, 0, j)),
            pl.BlockSpec((1, 1, 1, d), lambda l, j: (l, j, 0, 0)),
        ],
        out_specs=pl.BlockSpec((1, 1, r, d), lambda l, j: (l, j, 0, 0)),
        out_shape=jax.ShapeDtypeStruct((depth, N_MOD, r, d), F32),
        compiler_params=_cp(("parallel", "parallel")),
        name="ada_mod",
    )(c_all, ada_w, ada_b.reshape(depth, N_MOD, 1, d))


def _row_specs(bx, tm, d, mod, layer, sub):
    lm = mod.shape[3]
    tml = 1 if lm == 1 else tm
    x_spec = pl.BlockSpec((1, tm, d), lambda b, i, *_: (b, i, 0))
    mod_spec = pl.BlockSpec(
        (None, 3, None, tml, d), lambda b, i, *_: (layer, sub, b, 0 if lm == 1 else i, 0)
    )
    g_spec = pl.BlockSpec((None, 6, d), lambda b, i, *_: (layer, 0, 0))
    return x_spec, mod_spec, g_spec


def _pre(x, g_ref, m, sub):
    return _rms(x, g_ref[2 * sub : 2 * sub + 1, :]) * (1.0 + m[1]) + m[0]


def _post(x, o, g_ref, m, sub, w):
    return x + w * m[2] * _rms(o, g_ref[2 * sub + 1 : 2 * sub + 2, :])


def _ffn_kernel(x_ref, mod_ref, g_ref, wa_ref, wb_ref, wo_ref, y_ref, *, sub, parts):
    m = mod_ref[...]
    tm = x_ref.shape[1]
    rows = tm // parts
    for r in range(parts):
        sl = slice(r * rows, (r + 1) * rows)
        x = x_ref[0, sl, :]
        mr = m if m.shape[1] == 1 else m[:, sl, :]
        h = _pre(x, g_ref, mr, sub).astype(BF)
        u = _silu(_dot(h, wa_ref[...])) * _dot(h, wb_ref[...])
        o = _dot(u.astype(BF), wo_ref[...])
        y_ref[0, sl, :] = _post(x, o, g_ref, mr, sub, MACARON_W)


def _ffn_block(x, mod, norm_g, w_in, w_out, layer, sub, half, tm, parts):
    bx, L, d = x.shape
    dff = w_out.shape[2]
    x_spec, mod_spec, g_spec = _row_specs(bx, tm, d, mod, layer, sub)
    once = pl.Buffered(1)
    return pl.pallas_call(
        functools.partial(_ffn_kernel, sub=sub, parts=parts),
        grid=(bx, L // tm),
        in_specs=[
            x_spec,
            mod_spec,
            g_spec,
            pl.BlockSpec((None, None, d, dff), lambda b, i: (layer, half, 0, 0), pipeline_mode=once),
            pl.BlockSpec((None, None, d, dff), lambda b, i: (layer, half, 0, 1), pipeline_mode=once),
            pl.BlockSpec((None, None, dff, d), lambda b, i: (layer, half, 0, 0), pipeline_mode=once),
        ],
        out_specs=pl.BlockSpec((1, tm, d), lambda b, i: (b, i, 0)),
        out_shape=jax.ShapeDtypeStruct((bx, L, d), F32),
        compiler_params=_cp(("parallel", "parallel")),
        name="ffn_block",
    )(x, mod, norm_g, w_in, w_in, w_out)


def _conv_kernel(x_ref, mod_ref, g_ref, win_ref, ck_ref, wout_ref, p0_ref, p1_ref, y_ref, u_ref, u_scr, *, seq_mode, tm):
    d = x_ref.shape[-1]
    x = x_ref[0]
    m = mod_ref[...]
    h = _pre(x, g_ref, m, 1).astype(BF)
    z = _dot(h, win_ref[...])
    bg = z[:, 0:d]
    u = z[:, d : 2 * d] * z[:, 2 * d : 3 * d]
    ck = ck_ref[...]
    if seq_mode:
        @pl.when(pl.program_id(1) == 0)
        def _():
            u_scr[6:7, :] = p0_ref[0]
            u_scr[7:8, :] = p1_ref[0]

        u_scr[8 : 8 + tm, :] = u
        prev2 = u_scr[6 : 6 + tm, :]
        prev1 = u_scr[7 : 7 + tm, :]
        u_scr[6:8, :] = u[tm - 2 : tm, :]
        u_ref[0] = u[tm - 2 : tm, :]
    else:
        prev2 = p0_ref[0]
        prev1 = p1_ref[0]
        u_ref[0] = u
    y = ck[0:1, :] * prev2 + ck[1:2, :] * prev1 + ck[2:3, :] * u
    o = _dot((bg * y).astype(BF), wout_ref[...])
    y_ref[0] = _post(x, o, g_ref, m, 1, 1.0)


def _conv_block(x, mod, norm_g, w_in, conv_k, w_out, prev0, prev1, layer, tm, seq_mode):
    bx, L, d = x.shape
    x_spec, mod_spec, g_spec = _row_specs(bx, tm, d, mod, layer, 1)
    if seq_mode:
        p_spec = pl.BlockSpec((1, 1, d), lambda b, i: (b, 0, 0))
        u_spec = pl.BlockSpec((1, 2, d), lambda b, i: (b, 0, 0))
        u_shape = jax.ShapeDtypeStruct((bx, 2, d), F32)
    else:
        p_spec = pl.BlockSpec((1, tm, d), lambda b, i: (b, i, 0))
        u_spec = pl.BlockSpec((1, tm, d), lambda b, i: (b, i, 0))
        u_shape = jax.ShapeDtypeStruct((bx, L, d), F32)
    return pl.pallas_call(
        functools.partial(_conv_kernel, seq_mode=seq_mode, tm=tm),
        grid=(bx, L // tm),
        in_specs=[
            x_spec,
            mod_spec,
            g_spec,
            pl.BlockSpec((None, d, 3 * d), lambda b, i: (layer, 0, 0)),
            pl.BlockSpec((None, CONV_W, d), lambda b, i: (layer, 0, 0)),
            pl.BlockSpec((None, d, d), lambda b, i: (layer, 0, 0)),
            p_spec,
            p_spec,
        ],
        out_specs=[pl.BlockSpec((1, tm, d), lambda b, i: (b, i, 0)), u_spec],
        out_shape=[jax.ShapeDtypeStruct((bx, L, d), F32), u_shape],
        scratch_shapes=[pltpu.VMEM((tm + 8, d), F32)],
        compiler_params=_cp(("parallel", "arbitrary")),
        name="conv_block",
    )(x, mod, norm_g, w_in, conv_k, w_out, prev0, prev1)


def _kv_kernel(x_ref, g_ref, w_ref, cos_ref, sin_ref, rows_ref, win_ref):
    hn = _rms(x_ref[0], g_ref[...]).astype(BF)
    kv = _dot(hn, w_ref[...])
    cos = cos_ref[...]
    sin = sin_ref[...]
    rows_ref[0, :, 0 : 2 * KV_W] = kv[:, 0 : 2 * KV_W]
    rows_ref[0, :, 2 * KV_W : 3 * KV_W] = _rope(kv[:, 2 * KV_W : 3 * KV_W], cos, sin)
    rows_ref[0, :, 3 * KV_W : 4 * KV_W] = kv[:, 3 * KV_W : 4 * KV_W]
    win_ref[0, :, 0:KV_W] = _rope(kv[:, 4 * KV_W : 5 * KV_W], cos, sin)
    win_ref[0, :, KV_W : 2 * KV_W] = kv[:, 5 * KV_W : 6 * KV_W]


def _kv_proj(x, kv_norm_g, w_kv, cos, sin, tm):
    bx, L, d = x.shape
    return pl.pallas_call(
        _kv_kernel,
        grid=(bx, L // tm),
        in_specs=[
            pl.BlockSpec((1, tm, d), lambda b, i: (b, i, 0)),
            pl.BlockSpec((1, d), lambda b, i: (0, 0)),
            pl.BlockSpec((d, 6 * KV_W), lambda b, i: (0, 0)),
            pl.BlockSpec((tm, KV_W), lambda b, i: (i, 0)),
            pl.BlockSpec((tm, KV_W), lambda b, i: (i, 0)),
        ],
        out_specs=[
            pl.BlockSpec((1, tm, 4 * KV_W), lambda b, i: (b, i, 0)),
            pl.BlockSpec((1, tm, 2 * KV_W), lambda b, i: (b, i, 0)),
        ],
        out_shape=[
            jax.ShapeDtypeStruct((bx, L, 4 * KV_W), F32),
            jax.ShapeDtypeStruct((bx, L, 2 * KV_W), F32),
        ],
        compiler_params=_cp(("parallel", "parallel")),
        name="kv_proj",
    )(x, kv_norm_g.reshape(1, d), w_kv, cos, sin)


def _qg_kernel(x_ref, mod_ref, g_ref, wq_ref, wg_ref, cos_ref, sin_ref, qn_ref, qr_ref, gate_ref):
    h = _pre(x_ref[0], g_ref, mod_ref[...], 1).astype(BF)
    q = _dot(h, wq_ref[...])
    gate_ref[0] = jax.nn.sigmoid(_dot(h, wg_ref[...]))
    scale = HEAD_DIM**-0.5 * LOG2E
    qn_ref[0] = (q * scale).astype(BF)
    cos = cos_ref[...]
    sin = sin_ref[...]
    for c in range(q.shape[1] // KV_W):
        sl = slice(c * KV_W, (c + 1) * KV_W)
        qr_ref[0, :, sl] = (_rope(q[:, sl], cos, sin) * scale).astype(BF)


def _qg_proj(x, mod, norm_g, wq, wg, cos, sin, layer, tm):
    bx, L, d = x.shape
    nq = wq.shape[-1]
    x_spec, mod_spec, g_spec = _row_specs(bx, tm, d, mod, layer, 1)
    return pl.pallas_call(
        _qg_kernel,
        grid=(bx, L // tm),
        in_specs=[
            x_spec,
            mod_spec,
            g_spec,
            pl.BlockSpec((d, nq), lambda b, i: (0, 0)),
            pl.BlockSpec((d, 128), lambda b, i: (0, 0)),
            pl.BlockSpec((tm, KV_W), lambda b, i: (i, 0)),
            pl.BlockSpec((tm, KV_W), lambda b, i: (i, 0)),
        ],
        out_specs=[
            pl.BlockSpec((1, tm, nq), lambda b, i: (b, i, 0)),
            pl.BlockSpec((1, tm, nq), lambda b, i: (b, i, 0)),
            pl.BlockSpec((1, tm, 128), lambda b, i: (b, i, 0)),
        ],
        out_shape=[
            jax.ShapeDtypeStruct((bx, L, nq), BF),
            jax.ShapeDtypeStruct((bx, L, nq), BF),
            jax.ShapeDtypeStruct((bx, L, 128), F32),
        ],
        compiler_params=_cp(("parallel", "parallel")),
        name="qg_proj",
    )(x, mod, norm_g, wq, wg, cos, sin)


def _op_kernel(o_ref, x_ref, mod_ref, g_ref, w_ref, y_ref):
    a = _dot(o_ref[0], w_ref[...])
    y_ref[0] = _post(x_ref[0], a, g_ref, mod_ref[...], 1, 1.0)


def _out_proj(o, x, mod, norm_g, w_o, layer, tm):
    bx, L, d = x.shape
    x_spec, mod_spec, g_spec = _row_specs(bx, tm, d, mod, layer, 1)
    return pl.pallas_call(
        _op_kernel,
        grid=(bx, L // tm),
        in_specs=[
            pl.BlockSpec((1, tm, o.shape[-1]), lambda b, i: (b, i, 0)),
            x_spec,
            mod_spec,
            g_spec,
            pl.BlockSpec(w_o.shape, lambda b, i: (0, 0)),
        ],
        out_specs=pl.BlockSpec((1, tm, d), lambda b, i: (b, i, 0)),
        out_shape=jax.ShapeDtypeStruct((bx, L, d), F32),
        compiler_params=_cp(("parallel", "parallel")),
        name="out_proj",
    )(o, x, mod, norm_g, w_o)


def _compress_tile(get_x, pe_ref, w1_ref, w2_ref, b_scr, m_rows):
    half = CMP_LEN // 2
    xs = [get_x(l) for l in range(half)]
    xa = jnp.concatenate([(xs[l] + pe_ref[l]).astype(BF) for l in range(half)], axis=1)
    xb = jnp.concatenate([(xs[l] + pe_ref[half + l]).astype(BF) for l in range(half)], axis=1)
    a = _dot(xa, w1_ref[0])
    b_scr[0:m_rows, :] = _dot(xb, w1_ref[1])
    b_scr[m_rows : m_rows + 8, :] = jnp.zeros((8, b_scr.shape[1]), F32)
    hid = a + b_scr[1 : m_rows + 1, :]
    return _dot(_silu(hid).astype(BF), w2_ref[...])


def _cmp_prompt_kernel(x_ref, pe_ref, w1_ref, w2_ref, o_ref, b_scr, *, ng, nc):
    row = lax.broadcasted_iota(jnp.int32, (ng, 128), 0)
    res = _compress_tile(
        lambda l: x_ref[0, pl.ds(l, ng, stride=CMP_STRIDE), :], pe_ref, w1_ref, w2_ref, b_scr, ng
    )
    o_ref[0] = jnp.where(row < nc, res, 0.0)


def _cmp_weights(cmp_pe, cmp_w1, cmp_w2):
    z1 = jnp.zeros_like(cmp_w1)
    w1bd = jnp.concatenate(
        [jnp.concatenate([cmp_w1, z1], axis=3), jnp.concatenate([z1, cmp_w1], axis=3)], axis=2
    )
    hid2 = w1bd.shape[-1]
    w1bd = w1bd.reshape(2, 2, (CMP_LEN // 2) * 2 * HEAD_DIM, hid2).astype(BF)
    z2 = jnp.zeros_like(cmp_w2)
    w2bd = jnp.concatenate(
        [jnp.concatenate([cmp_w2, z2], axis=2), jnp.concatenate([z2, cmp_w2], axis=2)], axis=1
    ).astype(BF)
    pe2 = jnp.concatenate([cmp_pe, cmp_pe], axis=2)[:, :, None, :]
    return pe2, w1bd, w2bd


def _compress_prompt(rows, pe2, w1bd, w2bd):
    bx, L, _ = rows.shape
    ng = L // CMP_STRIDE
    nc = (L - CMP_LEN) // CMP_STRIDE + 1
    kdim, hid2 = w1bd.shape[2], w1bd.shape[3]
    return pl.pallas_call(
        functools.partial(_cmp_prompt_kernel, ng=ng, nc=nc),
        grid=(4, bx),
        in_specs=[
            pl.BlockSpec((1, L, 128), lambda cg, b: (b, 0, cg)),
            pl.BlockSpec((None, CMP_LEN, 1, 128), lambda cg, b: (cg // 2, 0, 0, 0)),
            pl.BlockSpec((None, 2, kdim, hid2), lambda cg, b: (cg // 2, 0, 0, 0)),
            pl.BlockSpec((None, hid2, 128), lambda cg, b: (cg // 2, 0, 0)),
        ],
        out_specs=pl.BlockSpec((1, ng, 128), lambda cg, b: (b, 0, cg)),
        out_shape=jax.ShapeDtypeStruct((bx, ng, 2 * KV_W), F32),
        scratch_shapes=[pltpu.VMEM((ng + 8, hid2), F32)],
        compiler_params=_cp(("parallel", "parallel")),
        name="compress_prompt",
    )(rows, pe2, w1bd, w2bd)


def _cmp_sample_kernel(pt_ref, *refs, n_pages, sb, gpp, nc):
    del pt_ref
    page_refs = refs[: sb * n_pages]
    pe_ref, w1_ref, w2_ref, o_ref, b_scr = refs[sb * n_pages :]
    ng = n_pages * gpp
    m_rows = sb * ng
    row = lax.broadcasted_iota(jnp.int32, (ng, 128), 0)

    def get_x(l):
        return jnp.concatenate([r[0, pl.ds(l, gpp, stride=CMP_STRIDE), :] for r in page_refs], axis=0)

    res = _compress_tile(get_x, pe_ref, w1_ref, w2_ref, b_scr, m_rows)
    for s in range(sb):
        o_ref[s] = jnp.where(row < nc, res[s * ng : (s + 1) * ng, :], 0.0)


def _compress_sample(cache2, page_table, pe2, w1bd, w2bd, total_len, sb):
    db, n_pages = page_table.shape
    page = cache2.shape[1]
    gpp = page // CMP_STRIDE
    ng = n_pages * gpp
    nc = (total_len - CMP_LEN) // CMP_STRIDE + 1
    assert nc <= ng
    kdim, hid2 = w1bd.shape[2], w1bd.shape[3]

    def page_spec(s, p):
        return pl.BlockSpec((1, page, 128), lambda cg, bb, pt: (pt[bb * sb + s, p], 0, cg))

    grid_spec = pltpu.PrefetchScalarGridSpec(
        num_scalar_prefetch=1,
        grid=(4, db // sb),
        in_specs=[page_spec(s, p) for s in range(sb) for p in range(n_pages)]
        + [
            pl.BlockSpec((None, CMP_LEN, 1, 128), lambda cg, bb, pt: (cg // 2, 0, 0, 0)),
            pl.BlockSpec((None, 2, kdim, hid2), lambda cg, bb, pt: (cg // 2, 0, 0, 0)),
            pl.BlockSpec((None, hid2, 128), lambda cg, bb, pt: (cg // 2, 0, 0)),
        ],
        out_specs=pl.BlockSpec((sb, ng, 128), lambda cg, bb, pt: (bb, 0, cg)),
        scratch_shapes=[pltpu.VMEM((sb * ng + 8, hid2), F32)],
    )
    return pl.pallas_call(
        functools.partial(_cmp_sample_kernel, n_pages=n_pages, sb=sb, gpp=gpp, nc=nc),
        grid_spec=grid_spec,
        out_shape=jax.ShapeDtypeStruct((db, ng, 2 * KV_W), F32),
        compiler_params=_cp(("parallel", "parallel")),
        name="compress_sample",
    )(page_table, *([cache2] * (sb * n_pages)), pe2, w1bd, w2bd)


def _select_scores(imp, blk, pos):
    cur = pos // SLC_BLOCK
    valid = blk * SLC_BLOCK <= pos
    forced = (blk == 0) | (blk == cur) | (blk == cur - 1)
    return jnp.where(valid, imp + FORCE_BONUS * forced.astype(F32), -1.0)


def _topk_rows(score, k):
    n = score.shape[0]
    ridx = lax.broadcasted_iota(jnp.int32, score.shape, 0).astype(F32)

    def body(_, sc):
        m = jnp.max(sc, axis=0, keepdims=True)
        first = jnp.min(jnp.where(sc == m, ridx, float(n)), axis=0, keepdims=True)
        return jnp.where(ridx == first, -jnp.inf, sc)

    sc = lax.fori_loop(0, k, body, score, unroll=True)
    return (sc == -jnp.inf) & (score >= 0.0)


BIAS_ROWS = 16


def _masked_attend(k_rows, q, bias, v_aug):
    s = _dot(k_rows, q) + jnp.concatenate([bias] * GROUP, axis=1)
    e = jnp.exp2(s - jnp.max(s, axis=0, keepdims=True))
    return e, _dot(v_aug, e.astype(BF))


def _attn_prompt_kernel(
    qn_ref, qr_ref, g_ref, kc_ref, vct_ref, ks_ref, vst_ref, kw_ref, vwt_ref, ovt_ref, ctab_ref, wtab_ref,
    o_ref, bias_scr, qaug_scr, sa_scr, sb_scr, m_scr, acc_scr, *, tq, tk, ns, top_n,
):
    start = pl.program_id(2) * tq
    gq = GROUP * tq
    bpc = tk // SLC_BLOCK

    def heads_on_lanes(ref):
        t = ref[0]
        return jnp.concatenate([t[HEAD_DIM * g : HEAD_DIM * (g + 1), :] for g in range(GROUP)], axis=1)

    qn = heads_on_lanes(qn_ref)
    qr = heads_on_lanes(qr_ref)
    pos = start + lax.broadcasted_iota(jnp.int32, (1, tq), 1)
    pos_g = jnp.concatenate([pos] * GROUP, axis=1)

    ngp = kc_ref.shape[2]
    c0 = pl.multiple_of(ngp - start // CMP_STRIDE, 8)
    e_c, un_c = _masked_attend(kc_ref[0, 0], qn, ctab_ref[pl.ds(c0, ngp), :], vct_ref[0, 0])
    inv_c = jnp.where(pos_g >= CMP_LEN - 1, 1.0 / jnp.maximum(un_c[HEAD_DIM : HEAD_DIM + 1, :], 1e-30), 0.0)
    o_c = un_c[0:HEAD_DIM, :] * inv_c
    psum = e_c[:, 0:tq] * inv_c[:, 0:tq]
    for g in range(1, GROUP):
        psum = psum + e_c[:, g * tq : (g + 1) * tq] * inv_c[:, g * tq : (g + 1) * tq]
    imp = _split_dot(ovt_ref[...], psum)

    blk = lax.broadcasted_iota(jnp.int32, (NS_PAD, 1), 0)
    score = jnp.where(blk < ns, _select_scores(imp, blk, pos), -jnp.inf)
    sel_bias = jnp.where(_topk_rows(score, top_n), 0.0, NEG)
    for c in range(NS_PAD // bpc):
        bias_scr[BIAS_ROWS * c : BIAS_ROWS * c + bpc, :] = sel_bias[bpc * c : bpc * (c + 1), :]
        if bpc < BIAS_ROWS:
            bias_scr[BIAS_ROWS * c + bpc : BIAS_ROWS * (c + 1), :] = jnp.zeros((BIAS_ROWS - bpc, tq), F32)

    wk = WINDOW + tq
    w0 = pl.multiple_of(jnp.maximum(start - WINDOW, 0), 128)
    t0 = pl.multiple_of(WINDOW - (start - w0), 128)
    _, un_w = _masked_attend(
        kw_ref[0, 0, pl.ds(w0, wk), :], qr, wtab_ref[pl.ds(t0, wk), :], vwt_ref[0, 0, :, pl.ds(w0, wk)]
    )
    o_w = un_w[0:HEAD_DIM, :] * (1.0 / jnp.maximum(un_w[HEAD_DIM : HEAD_DIM + 1, :], 1e-30))

    qaug_scr[0:HEAD_DIM, :] = qr
    qaug_scr[HEAD_DIM:, :] = jnp.zeros((qaug_scr.shape[0] - HEAD_DIM, gq), BF)
    m_scr[...] = jnp.full(m_scr.shape, NEG, F32)
    acc_scr[...] = jnp.zeros(acc_scr.shape, F32)

    def scores(c, s_ref):
        k0 = pl.multiple_of(c * tk, tk)
        b = bias_scr[pl.ds(pl.multiple_of(c * BIAS_ROWS, BIAS_ROWS), BIAS_ROWS), :]
        qaug_scr[HEAD_DIM : HEAD_DIM + BIAS_ROWS, :] = jnp.concatenate([b] * GROUP, axis=1).astype(BF)
        s_ref[...] = _dot(ks_ref[0, 0, pl.ds(k0, tk), :], qaug_scr[...])

    def accumulate(c, s_ref, causal):
        k0 = pl.multiple_of(c * tk, tk)
        s = s_ref[...]
        if causal:
            kpos = k0 + lax.broadcasted_iota(jnp.int32, (tk, 1), 0)
            s = s + jnp.concatenate([jnp.where(kpos <= pos, 0.0, NEG)] * GROUP, axis=1)
        m_old = m_scr[...]
        m_new = jnp.maximum(m_old, jnp.max(s, axis=0, keepdims=True))
        p = jnp.exp2(s - m_new).astype(BF)
        acc_scr[...] = jnp.exp2(m_old - m_new) * acc_scr[...] + _dot(vst_ref[0, 0, :, pl.ds(k0, tk)], p)
        m_scr[...] = m_new

    def pair(j, carry):
        scores(2 * j + 1, sb_scr)
        accumulate(2 * j, sa_scr, False)
        scores(2 * j + 2, sa_scr)
        accumulate(2 * j + 1, sb_scr, False)
        return carry

    last = (start + tq - 1) // tk
    scores(0, sa_scr)
    lax.fori_loop(0, last // 2, pair, 0)

    @pl.when(last % 2 == 0)
    def _():
        accumulate(last, sa_scr, True)

    @pl.when(last % 2 == 1)
    def _():
        scores(last, sb_scr)
        accumulate(last - 1, sa_scr, False)
        accumulate(last, sb_scr, True)

    acc = acc_scr[...]
    o_s = acc[0:HEAD_DIM, :] * (1.0 / jnp.maximum(acc[HEAD_DIM : HEAD_DIM + 1, :], 1e-30))

    gates = g_ref[0, 0]
    outs = []
    for g in range(GROUP):
        sl = slice(g * tq, (g + 1) * tq)
        outs.append(
            gates[3 * g : 3 * g + 1, :] * o_c[:, sl]
            + gates[3 * g + 1 : 3 * g + 2, :] * o_s[:, sl]
            + gates[3 * g + 2 : 3 * g + 3, :] * o_w[:, sl]
        )
    o_ref[0] = jnp.concatenate(outs, axis=0).astype(BF)


def _attn_prompt(qn_t, qr_t, gates_t, kc, vc_t, ks, vs_t, kw, vw_t, ov_t, ctab, wtab, ns, tq, tk):
    bx, _, L = qn_t.shape
    ngp = kc.shape[2]
    gd = GROUP * HEAD_DIM
    kaug = ks.shape[-1]
    vaug = vs_t.shape[2]
    kv_spec = pl.BlockSpec((1, 1, L, HEAD_DIM), lambda b, k, i: (b, k, 0, 0))
    vt_spec = pl.BlockSpec((1, 1, vaug, L), lambda b, k, i: (b, k, 0, 0))
    q_spec = pl.BlockSpec((1, gd, tq), lambda b, k, i: (b, k, i))
    return pl.pallas_call(
        functools.partial(_attn_prompt_kernel, tq=tq, tk=tk, ns=ns, top_n=min(TOP_N, ns)),
        grid=(bx, N_KV_HEADS, L // tq),
        in_specs=[
            q_spec,
            q_spec,
            pl.BlockSpec((1, 1, 16, tq), lambda b, k, i: (b, k, 0, i)),
            pl.BlockSpec((1, 1, ngp, HEAD_DIM), lambda b, k, i: (b, k, 0, 0)),
            pl.BlockSpec((1, 1, vaug, ngp), lambda b, k, i: (b, k, 0, 0)),
            pl.BlockSpec((1, 1, L, kaug), lambda b, k, i: (b, k, 0, 0)),
            vt_spec,
            kv_spec,
            vt_spec,
            pl.BlockSpec(ov_t.shape, lambda b, k, i: (0, 0)),
            pl.BlockSpec(ctab.shape, lambda b, k, i: (0, 0)),
            pl.BlockSpec(wtab.shape, lambda b, k, i: (0, 0)),
        ],
        out_specs=q_spec,
        out_shape=jax.ShapeDtypeStruct(qn_t.shape, BF),
        scratch_shapes=[
            pltpu.VMEM((NS_PAD // (tk // SLC_BLOCK) * BIAS_ROWS, tq), F32),
            pltpu.VMEM((kaug, GROUP * tq), BF),
            pltpu.VMEM((tk, GROUP * tq), F32),
            pltpu.VMEM((tk, GROUP * tq), F32),
            pltpu.VMEM((1, GROUP * tq), F32),
            pltpu.VMEM((vaug, GROUP * tq), F32),
        ],
        compiler_params=_cp(("parallel", "parallel", "arbitrary")),
        name="attn_prompt",
    )(qn_t, qr_t, gates_t, kc, vc_t, ks, vs_t, kw, vw_t, ov_t, ctab, wtab)


def _softmax_rows_ext(s, mask, s_new, m_new_mask):
    sm = jnp.where(mask, s, NEG)
    sn = jnp.where(m_new_mask, s_new, NEG)
    mx = jnp.maximum(jnp.max(sm, axis=1, keepdims=True), sn)
    e = jnp.exp2(sm - mx) * mask.astype(F32)
    en = jnp.exp2(sn - mx) * m_new_mask.astype(F32)
    inv = 1.0 / jnp.maximum(jnp.sum(e, axis=1, keepdims=True) + en, 1e-30)
    return e * inv, en * inv


def _attn_sample_kernel(
    pt_ref, *refs, n_pages, page, past_len, nc, ns, top_n, win_buf,
):
    del pt_ref
    page_refs = refs[:n_pages]
    (qn_ref, qr_ref, g_ref, cmp_ref, new_ref, win_ref, wnew_ref, ov_ref, exp_ref, gsum_ref, o_ref) = refs[n_pages:]
    pos = past_len
    qn = qn_ref[0]
    qr = qr_ref[0]
    qr32 = qr.astype(F32)
    nh = qn.shape[0]

    cmp = cmp_ref[0]
    kc = cmp[:, 0:KV_W].astype(BF)
    vc = cmp[:, KV_W : 2 * KV_W].astype(BF)
    ngp = kc.shape[0]
    n_idx = lax.broadcasted_iota(jnp.int32, (1, ngp), 1)
    cmask = jnp.broadcast_to((n_idx * CMP_STRIDE + (CMP_LEN - 1) <= pos) & (n_idx < nc), (nh, ngp))
    sm = jnp.where(cmask, _dot_nt(qn, kc), NEG)
    e = jnp.exp2(sm - jnp.max(sm, axis=1, keepdims=True)) * cmask.astype(F32)
    p_c = e / jnp.maximum(jnp.sum(e, axis=1, keepdims=True), 1e-30)
    o_c = _dot(p_c.astype(BF), vc)

    t = _split_dot_r(p_c, ov_ref[...])
    imp = _split_dot(gsum_ref[...], t)
    blk = lax.broadcasted_iota(jnp.int32, (1, NS_PAD), 1)
    score = jnp.where(blk < ns, _select_scores(imp, blk, pos), -jnp.inf)
    pad = jnp.concatenate([score, jnp.zeros((NS_PAD - nh, NS_PAD), F32)], axis=0)
    score_t = pad.T
    ii = lax.broadcasted_iota(jnp.int32, (NS_PAD, NS_PAD), 0)
    jj = lax.broadcasted_iota(jnp.int32, (NS_PAD, NS_PAD), 1)
    head_grp = lax.broadcasted_iota(jnp.int32, (nh, NS_PAD), 0) // GROUP
    sel = jnp.zeros((nh, NS_PAD), F32)
    for k in range(nh // GROUP):
        h = k * GROUP
        a = score_t[:, h : h + 1]
        b = score[h : h + 1, :]
        beats = (a > b) | ((a == b) & (ii < jj))
        rank = jnp.sum(beats.astype(F32), axis=0, keepdims=True)
        sel_k = jnp.where((rank < float(top_n)) & (b >= 0.0), 1.0, 0.0)
        sel = jnp.where(head_grp == k, sel_k, sel)

    kmask = _dot(sel.astype(BF), exp_ref[...]) > 0.5
    new = new_ref[0]
    k_new = new[:, 2 * KV_W : 3 * KV_W]
    v_new = new[:, 3 * KV_W : 4 * KV_W]
    s_new = jnp.sum(qr32 * k_new.astype(BF).astype(F32), axis=1, keepdims=True)
    nb = pos // SLC_BLOCK
    new_sel = sel[:, nb : nb + 1] > 0.5
    s = jnp.concatenate([_dot_nt(qr, r[0][:, 0:KV_W].astype(BF)) for r in page_refs], axis=1)
    p_s, p_new = _softmax_rows_ext(s, kmask, s_new, new_sel)
    o_s = p_new * v_new.astype(BF).astype(F32)
    for i, r in enumerate(page_refs):
        o_s = o_s + _dot(p_s[:, i * page : (i + 1) * page].astype(BF), r[0][:, KV_W : 2 * KV_W].astype(BF))

    wbuf = win_ref[0]
    wnew = wnew_ref[0]
    kw_pos = (past_len - win_buf) + lax.broadcasted_iota(jnp.int32, (1, win_buf), 1)
    rel = pos - kw_pos
    wmask = jnp.broadcast_to((rel >= 0) & (rel < WINDOW) & (kw_pos >= 0), (nh, win_buf))
    s_w = _dot_nt(qr, wbuf[:, 0:KV_W].astype(BF))
    s_wn = jnp.sum(qr32 * wnew[:, 0:KV_W].astype(BF).astype(F32), axis=1, keepdims=True)
    p_w, p_wn = _softmax_rows_ext(s_w, wmask, s_wn, jnp.full((nh, 1), True))
    o_w = _dot(p_w.astype(BF), wbuf[:, KV_W : 2 * KV_W].astype(BF)) + p_wn * wnew[:, KV_W : 2 * KV_W].astype(BF).astype(F32)

    gates = g_ref[0]
    o_ref[0] = gates[:, 0:1] * o_c + gates[:, 1:2] * o_s + gates[:, 2:3] * o_w


def _split_dot_r(a_f32, b):
    hi = a_f32.astype(BF)
    lo = (a_f32 - hi.astype(F32)).astype(BF)
    return _dot(hi, b) + _dot(lo, b)


def _attn_sample(cache2, page_table, qn_bd, qr_bd, gates, cmp_s, rows_new, win_state, win_new, ov, expand, gsum, nc, ns):
    db, n_pages = page_table.shape
    page = cache2.shape[1]
    past_len = n_pages * page
    win_buf = win_state.shape[1]
    ngp = cmp_s.shape[1]

    def one(shape):
        return pl.BlockSpec((1,) + shape, lambda b, pt: (b, 0, 0))

    def const(a):
        return pl.BlockSpec(a.shape, lambda b, pt: (0, 0))

    grid_spec = pltpu.PrefetchScalarGridSpec(
        num_scalar_prefetch=1,
        grid=(db,),
        in_specs=[pl.BlockSpec((1, page, 2 * KV_W), (lambda b, pt, p=p: (pt[b, p], 0, 1))) for p in range(n_pages)]
        + [
            one((N_HEADS, KV_W)),
            one((N_HEADS, KV_W)),
            one((N_HEADS, 3)),
            one((ngp, 2 * KV_W)),
            one((1, 4 * KV_W)),
            one((win_buf, 2 * KV_W)),
            one((1, 2 * KV_W)),
            const(ov),
            const(expand),
            const(gsum),
        ],
        out_specs=one((N_HEADS, KV_W)),
    )
    return pl.pallas_call(
        functools.partial(
            _attn_sample_kernel, n_pages=n_pages, page=page, past_len=past_len, nc=nc, ns=ns,
            top_n=min(TOP_N, ns), win_buf=win_buf,
        ),
        grid_spec=grid_spec,
        out_shape=jax.ShapeDtypeStruct((db, N_HEADS, KV_W), F32),
        compiler_params=_cp(("parallel",)),
        name="attn_sample",
    )(page_table, *([cache2] * n_pages), qn_bd, qr_bd, gates, cmp_s, rows_new, win_state, win_new, ov, expand, gsum)


def _rope_tables(pos):
    half = HEAD_DIM // 2
    inv = ROPE_THETA ** (-jnp.arange(half, dtype=F32) / half)
    ang = pos.astype(F32)[:, None] * inv[None]
    cos, sin = jnp.cos(ang), jnp.sin(ang)
    cos = jnp.tile(jnp.concatenate([cos, cos], axis=1), (1, N_KV_HEADS))
    sin = jnp.tile(jnp.concatenate([-sin, sin], axis=1), (1, N_KV_HEADS))
    return cos, sin


def _overlap(n_rows, n_cols):
    cs = jnp.arange(n_rows)[:, None] * CMP_STRIDE
    ss = jnp.arange(n_cols)[None] * SLC_BLOCK
    return ((cs < ss + SLC_BLOCK) & (cs + CMP_LEN > ss)).astype(F32)


def _pick_tile(n, pref):
    t = min(n, pref)
    while n % t:
        t //= 2
    return t


def kernel(x_prompt, x_sample, cache_kv, state_win_kv, state_conv, page_table, c_prompt, c_sample,
           ada_w, ada_b, norm_g, ffn_w_in, ffn_w_out, conv_w_in, conv_k, conv_w_out,
           kv_norm_g, w_kv, cmp_pe, cmp_w1, cmp_w2, nsa_w_qg, nsa_w_o):
    bp, seq, d = x_prompt.shape
    db, dec_seq, _ = x_sample.shape
    assert dec_seq == 1
    depth = ada_w.shape[0]
    n_a = conv_w_in.shape[0]
    n_pages = page_table.shape[1]
    page = cache_kv.shape[1]
    past_len = n_pages * page
    win_buf = state_win_kv.shape[1]
    dff = ffn_w_out.shape[2]
    nq = N_HEADS * HEAD_DIM

    ffn_w_in_b = ffn_w_in.astype(BF)
    ffn_w_out_b = ffn_w_out.astype(BF)
    conv_w_in_b = conv_w_in.astype(BF)
    conv_w_out_b = conv_w_out.astype(BF)
    w_kv_b = w_kv.astype(BF)
    wq_b = nsa_w_qg[:, :, :nq].astype(BF)
    wg_b = jnp.pad(nsa_w_qg[:, :, nq:], ((0, 0), (0, 0), (0, 128 - 3 * N_HEADS))).astype(BF)
    w_o_b = nsa_w_o.astype(BF)
    pe2, w1bd, w2bd = _cmp_weights(cmp_pe, cmp_w1, cmp_w2)

    r_pad = -(db + bp) % 8
    c_all = jnp.concatenate([c_sample, c_prompt, jnp.zeros((r_pad, d), F32)], axis=0)
    mod_all = _ada_mod(c_all, ada_w, ada_b)
    mod_s = mod_all[:, :, None, :db, :]
    mod_p = mod_all[:, :, db : db + bp, None, :]

    def trunk(x, mod, tm, conv_prev, seq_mode, build_kv, attend):
        conv_states = []
        parts = 2 if tm % 512 == 0 else 1
        for l in range(depth):
            x = _ffn_block(x, mod, norm_g, ffn_w_in_b, ffn_w_out_b, l, 0, 0, tm, parts)
            if l < n_a:
                p0, p1 = conv_prev(l)
                x, st = _conv_block(x, mod, norm_g, conv_w_in_b, conv_k, conv_w_out_b, p0, p1, l, tm, seq_mode)
                conv_states.append(st)
            else:
                j = l - n_a
                x = attend(x, mod, l, j)
            x = _ffn_block(x, mod, norm_g, ffn_w_in_b, ffn_w_out_b, l, 2, 1, tm, parts)
            if l == n_a - 1:
                build_kv(x)
        return x, conv_states

    tm_p = _pick_tile(seq, 512)
    cos_p, sin_p = _rope_tables(jnp.arange(seq, dtype=jnp.int32))
    ng_p = seq // CMP_STRIDE
    ns_p = -(-seq // SLC_BLOCK)
    assert ns_p <= NS_PAD and seq % Q_BLOCK == 0 and seq >= WINDOW + Q_BLOCK
    ov_t_p = _overlap(ng_p, NS_PAD).T.astype(BF)
    tk_p = _pick_tile(seq, BIAS_ROWS * SLC_BLOCK)
    qoff = jnp.arange(Q_BLOCK)[None, :]
    cu = jnp.arange(2 * ng_p)[:, None] - ng_p
    ctab_p = jnp.where(cu * CMP_STRIDE + (CMP_LEN - 1) <= qoff, 0.0, NEG).astype(F32)
    wu = jnp.arange(2 * WINDOW + Q_BLOCK)[:, None]
    wtab_p = jnp.where((wu > qoff) & (wu <= qoff + WINDOW), 0.0, NEG).astype(F32)
    ctx_p = {}

    def build_prompt(x):
        rows, win = _kv_proj(x, kv_norm_g, w_kv_b, cos_p, sin_p, tm_p)
        cmp = _compress_prompt(rows, pe2, w1bd, w2bd)

        def heads_major(a):
            return a.reshape(bp, a.shape[1], N_KV_HEADS, HEAD_DIM).transpose(0, 2, 1, 3).astype(BF)

        def values_t(a):
            n = a.shape[1]
            ones_rows = jnp.broadcast_to((jnp.arange(BIAS_ROWS)[:, None] == 0).astype(BF), (BIAS_ROWS, n))
            return jnp.concatenate(
                [
                    a.transpose(0, 2, 1).astype(BF).reshape(bp, N_KV_HEADS, HEAD_DIM, n),
                    jnp.broadcast_to(ones_rows, (bp, N_KV_HEADS, BIAS_ROWS, n)),
                ],
                axis=2,
            )

        blk_in_chunk = (jnp.arange(seq) % tk_p) // SLC_BLOCK
        onehot = (blk_in_chunk[:, None] == jnp.arange(128 - HEAD_DIM)[None, :]).astype(BF)
        ks = jnp.concatenate(
            [heads_major(rows[:, :, 2 * KV_W : 3 * KV_W]), jnp.broadcast_to(onehot, (bp, N_KV_HEADS) + onehot.shape)],
            axis=3,
        )
        ctx_p.update(
            rows=rows, win=win,
            kc=heads_major(cmp[:, :, :KV_W]), vc_t=values_t(cmp[:, :, KV_W:]),
            ks=ks, vs_t=values_t(rows[:, :, 3 * KV_W :]),
            kw=heads_major(win[:, :, :KV_W]), vw_t=values_t(win[:, :, KV_W:]),
        )

    def attend_prompt(x, mod, l, j):
        qn, qr, gates = _qg_proj(x, mod, norm_g, wq_b[j], wg_b[j], cos_p, sin_p, l, tm_p)
        gates_t = jnp.pad(
            gates[:, :, : 3 * N_HEADS].reshape(bp, seq, N_KV_HEADS, 3 * GROUP), ((0, 0), (0, 0), (0, 0), (0, 16 - 3 * GROUP))
        ).transpose(0, 2, 3, 1)
        o_t = _attn_prompt(
            qn.transpose(0, 2, 1), qr.transpose(0, 2, 1), gates_t, ctx_p["kc"], ctx_p["vc_t"], ctx_p["ks"],
            ctx_p["vs_t"], ctx_p["kw"], ctx_p["vw_t"], ov_t_p, ctab_p, wtab_p, ns_p, Q_BLOCK, tk_p,
        )
        return _out_proj(o_t.transpose(0, 2, 1), x, mod, norm_g, w_o_b[j], l, tm_p)

    zero_row = jnp.zeros((bp, 1, d), F32)
    y_prompt, conv_p = trunk(x_prompt, mod_p, tm_p, lambda l: (zero_row, zero_row), True, build_prompt, attend_prompt)
    kv_rows_p = ctx_p["rows"].reshape(bp, seq, 4, N_KV_HEADS, HEAD_DIM)
    wbp = min(WINDOW, seq)
    win_p = ctx_p["win"][:, seq - wbp :].reshape(bp, wbp, 2, N_KV_HEADS, HEAD_DIM)
    conv_p = jnp.stack(conv_p)

    xs = x_sample.reshape(1, db, d)
    tm_s = db
    cos_s, sin_s = _rope_tables(jnp.full((db,), past_len, jnp.int32))
    total = past_len + 1
    nc_s = (total - CMP_LEN) // CMP_STRIDE + 1
    ns_s = -(-total // SLC_BLOCK)
    ng_s = past_len // CMP_STRIDE
    assert ns_s <= NS_PAD and ng_s <= 128 and past_len // SLC_BLOCK < ns_s
    cache2 = cache_kv.reshape(cache_kv.shape[0], page, 4 * KV_W)
    win_state = state_win_kv.reshape(db, win_buf, 2 * KV_W)
    ov_s = _overlap(ng_s, NS_PAD).astype(BF)
    expand = (jnp.arange(past_len)[None, :] // SLC_BLOCK == jnp.arange(NS_PAD)[:, None]).astype(BF)
    head_kv = jnp.arange(N_HEADS) // GROUP
    gsum = (head_kv[:, None] == head_kv[None, :]).astype(BF)
    bd_mask = (head_kv[:, None] == jnp.arange(N_KV_HEADS)[None, :]).astype(BF)
    ctx_s = {}

    def build_sample(x):
        rows, win = _kv_proj(x, kv_norm_g, w_kv_b, cos_s, sin_s, tm_s)
        ctx_s.update(
            rows=rows.reshape(db, 1, 4 * KV_W), win=win.reshape(db, 1, 2 * KV_W),
            cmp=_compress_sample(cache2, page_table, pe2, w1bd, w2bd, total, _pick_tile(db, 4)),
        )

    def block_diag(q):
        qh = q.reshape(db, N_HEADS, 1, HEAD_DIM) * bd_mask[None, :, :, None]
        return qh.reshape(db, N_HEADS, KV_W)

    def attend_sample(x, mod, l, j):
        qn, qr, gates = _qg_proj(x, mod, norm_g, wq_b[j], wg_b[j], cos_s, sin_s, l, tm_s)
        o_full = _attn_sample(
            cache2, page_table, block_diag(qn[0]), block_diag(qr[0]), gates[0, :, : 3 * N_HEADS].reshape(db, N_HEADS, 3),
            ctx_s["cmp"], ctx_s["rows"], win_state, ctx_s["win"], ov_s, expand, gsum, nc_s, ns_s,
        )
        o = jnp.take_along_axis(
            o_full.reshape(db, N_HEADS, N_KV_HEADS, HEAD_DIM), head_kv[None, :, None, None], axis=2
        ).reshape(1, db, nq)
        return _out_proj(o.astype(BF), x, mod, norm_g, w_o_b[j], l, tm_s)

    def conv_prev_s(l):
        return state_conv[l, :, 0, :].reshape(1, db, d), state_conv[l, :, 1, :].reshape(1, db, d)

    y_s, conv_u = trunk(xs, mod_s, tm_s, conv_prev_s, False, build_sample, attend_sample)
    y_sample = y_s.reshape(db, 1, d)
    kv_rows_s = ctx_s["rows"].reshape(db, 1, 4, N_KV_HEADS, HEAD_DIM)
    win_s = jnp.concatenate([state_win_kv, ctx_s["win"].reshape(db, 1, 2, N_KV_HEADS, HEAD_DIM)], axis=1)[:, 1:]
    conv_s = jnp.stack([jnp.stack([state_conv[l, :, 1, :], conv_u[l][0]], axis=1) for l in range(n_a)])

    return (y_prompt, y_sample, kv_rows_p, win_p, conv_p, kv_rows_s, win_s, conv_s)
```

```python
import functools

import jax
import jax.numpy as jnp
from jax import lax
from jax.experimental import pallas as pl
from jax.experimental.pallas import tpu as pltpu

F32 = jnp.float32
BF = jnp.bfloat16

HEAD_DIM = 64
N_HEADS = 16
N_KV_HEADS = 4
GROUP = N_HEADS // N_KV_HEADS
KV_W = N_KV_HEADS * HEAD_DIM
CMP_LEN = 32
CMP_STRIDE = 16
SLC_BLOCK = 64
TOP_N = 16
WINDOW = 512
Q_BLOCK = 256
CONV_W = 3
MACARON_W = 0.5
ROPE_THETA = 10000.0
EPS = 1e-6
FORCE_BONUS = 1e3
LOG2E = 1.4426950408889634
NEG = -1e30
N_MOD = 9
NS_PAD = 128
VMEM_LIMIT = 52 * 1024 * 1024


def _cp(sem):
    return pltpu.CompilerParams(dimension_semantics=sem, vmem_limit_bytes=VMEM_LIMIT)


def _dot(a, b):
    return jnp.dot(a, b, preferred_element_type=F32)


def _dot_nt(a, b):
    return lax.dot_general(a, b, (((1,), (1,)), ((), ())), preferred_element_type=F32)


def _split_dot(a, b_f32):
    hi = b_f32.astype(BF)
    lo = (b_f32 - hi.astype(F32)).astype(BF)
    return _dot(a, hi) + _dot(a, lo)


def _rms(x, g):
    return x * lax.rsqrt(jnp.mean(x * x, axis=-1, keepdims=True) + EPS) * g


def _silu(a):
    return a * jax.nn.sigmoid(a)


def _rope(x, cos, sin_signed):
    n = x.shape[-1]
    lane = lax.broadcasted_iota(jnp.int32, x.shape, 1)
    first = (lane & (HEAD_DIM // 2)) == 0
    partner = jnp.where(first, pltpu.roll(x, n - HEAD_DIM // 2, axis=1), pltpu.roll(x, HEAD_DIM // 2, axis=1))
    return x * cos + partner * sin_signed


def _ada_kernel(c_ref, w_ref, b_ref, o_ref):
    cond = _silu(c_ref[...]).astype(BF)
    o_ref[0, 0] = _dot(cond, w_ref[0].astype(BF)) + b_ref[0, 0]


def _ada_mod(c_all, ada_w, ada_b):
    depth, d, _ = ada_w.shape
    r = c_all.shape[0]
    return pl.pallas_call(
        _ada_kernel,
        grid=(depth, N_MOD),
        in_specs=[
            pl.BlockSpec((r, d), lambda l, j: (0, 0)),
            pl.BlockSpec((1, d, d), lambda l, j: (l, 0, j)),
            pl.BlockSpec((1, 1, 1, d), lambda l, j: (l, j, 0, 0)),
        ],
        out_specs=pl.BlockSpec((1, 1, r, d), lambda l, j: (l, j, 0, 0)),
        out_shape=jax.ShapeDtypeStruct((depth, N_MOD, r, d), F32),
        compiler_params=_cp(("parallel", "parallel")),
        name="ada_mod",
    )(c_all, ada_w, ada_b.reshape(depth, N_MOD, 1, d))


def _row_specs(bx, tm, d, mod, layer, sub):
    lm = mod.shape[3]
    tml = 1 if lm == 1 else tm
    x_spec = pl.BlockSpec((1, tm, d), lambda b, i, *_: (b, i, 0))
    mod_spec = pl.BlockSpec(
        (None, 3, None, tml, d), lambda b, i, *_: (layer, sub, b, 0 if lm == 1 else i, 0)
    )
    g_spec = pl.BlockSpec((None, 6, d), lambda b, i, *_: (layer, 0, 0))
    return x_spec, mod_spec, g_spec


def _pre(x, g_ref, m, sub):
    return _rms(x, g_ref[2 * sub : 2 * sub + 1, :]) * (1.0 + m[1]) + m[0]


def _post(x, o, g_ref, m, sub, w):
    return x + w * m[2] * _rms(o, g_ref[2 * sub + 1 : 2 * sub + 2, :])


def _ffn_kernel(x_ref, mod_ref, g_ref, wa_ref, wb_ref, wo_ref, y_ref, *, sub, parts):
    m = mod_ref[...]
    tm = x_ref.shape[1]
    rows = tm // parts
    for r in range(parts):
        sl = slice(r * rows, (r + 1) * rows)
        x = x_ref[0, sl, :]
        mr = m if m.shape[1] == 1 else m[:, sl, :]
        h = _pre(x, g_ref, mr, sub).astype(BF)
        u = _silu(_dot(h, wa_ref[...])) * _dot(h, wb_ref[...])
        o = _dot(u.astype(BF), wo_ref[...])
        y_ref[0, sl, :] = _post(x, o, g_ref, mr, sub, MACARON_W)


def _ffn_block(x, mod, norm_g, w_in, w_out, layer, sub, half, tm, parts):
    bx, L, d = x.shape
    dff = w_out.shape[2]
    x_spec, mod_spec, g_spec = _row_specs(bx, tm, d, mod, layer, sub)
    once = pl.Buffered(1)
    return pl.pallas_call(
        functools.partial(_ffn_kernel, sub=sub, parts=parts),
        grid=(bx, L // tm),
        in_specs=[
            x_spec,
            mod_spec,
            g_spec,
            pl.BlockSpec((None, None, d, dff), lambda b, i: (layer, half, 0, 0), pipeline_mode=once),
            pl.BlockSpec((None, None, d, dff), lambda b, i: (layer, half, 0, 1), pipeline_mode=once),
            pl.BlockSpec((None, None, dff, d), lambda b, i: (layer, half, 0, 0), pipeline_mode=once),
        ],
        out_specs=pl.BlockSpec((1, tm, d), lambda b, i: (b, i, 0)),
        out_shape=jax.ShapeDtypeStruct((bx, L, d), F32),
        compiler_params=_cp(("parallel", "parallel")),
        name="ffn_block",
    )(x, mod, norm_g, w_in, w_in, w_out)


def _conv_kernel(x_ref, mod_ref, g_ref, win_ref, ck_ref, wout_ref, p0_ref, p1_ref, y_ref, u_ref, u_scr, *, seq_mode, tm):
    d = x_ref.shape[-1]
    x = x_ref[0]
    m = mod_ref[...]
    h = _pre(x, g_ref, m, 1).astype(BF)
    z = _dot(h, win_ref[...])
    bg = z[:, 0:d]
    u = z[:, d : 2 * d] * z[:, 2 * d : 3 * d]
    ck = ck_ref[...]
    if seq_mode:
        @pl.when(pl.program_id(1) == 0)
        def _():
            u_scr[6:7, :] = p0_ref[0]
            u_scr[7:8, :] = p1_ref[0]

        u_scr[8 : 8 + tm, :] = u
        prev2 = u_scr[6 : 6 + tm, :]
        prev1 = u_scr[7 : 7 + tm, :]
        u_scr[6:8, :] = u[tm - 2 : tm, :]
        u_ref[0] = u[tm - 2 : tm, :]
    else:
        prev2 = p0_ref[0]
        prev1 = p1_ref[0]
        u_ref[0] = u
    y = ck[0:1, :] * prev2 + ck[1:2, :] * prev1 + ck[2:3, :] * u
    o = _dot((bg * y).astype(BF), wout_ref[...])
    y_ref[0] = _post(x, o, g_ref, m, 1, 1.0)


def _conv_block(x, mod, norm_g, w_in, conv_k, w_out, prev0, prev1, layer, tm, seq_mode):
    bx, L, d = x.shape
    x_spec, mod_spec, g_spec = _row_specs(bx, tm, d, mod, layer, 1)
    if seq_mode:
        p_spec = pl.BlockSpec((1, 1, d), lambda b, i: (b, 0, 0))
        u_spec = pl.BlockSpec((1, 2, d), lambda b, i: (b, 0, 0))
        u_shape = jax.ShapeDtypeStruct((bx, 2, d), F32)
    else:
        p_spec = pl.BlockSpec((1, tm, d), lambda b, i: (b, i, 0))
        u_spec = pl.BlockSpec((1, tm, d), lambda b, i: (b, i, 0))
        u_shape = jax.ShapeDtypeStruct((bx, L, d), F32)
    return pl.pallas_call(
        functools.partial(_conv_kernel, seq_mode=seq_mode, tm=tm),
        grid=(bx, L // tm),
        in_specs=[
            x_spec,
            mod_spec,
            g_spec,
            pl.BlockSpec((None, d, 3 * d), lambda b, i: (layer, 0, 0)),
            pl.BlockSpec((None, CONV_W, d), lambda b, i: (layer, 0, 0)),
            pl.BlockSpec((None, d, d), lambda b, i: (layer, 0, 0)),
            p_spec,
            p_spec,
        ],
        out_specs=[pl.BlockSpec((1, tm, d), lambda b, i: (b, i, 0)), u_spec],
        out_shape=[jax.ShapeDtypeStruct((bx, L, d), F32), u_shape],
        scratch_shapes=[pltpu.VMEM((tm + 8, d), F32)],
        compiler_params=_cp(("parallel", "arbitrary")),
        name="conv_block",
    )(x, mod, norm_g, w_in, conv_k, w_out, prev0, prev1)


def _kv_kernel(x_ref, g_ref, w_ref, cos_ref, sin_ref, rows_ref, win_ref):
    hn = _rms(x_ref[0], g_ref[...]).astype(BF)
    kv = _dot(hn, w_ref[...])
    cos = cos_ref[...]
    sin = sin_ref[...]
    rows_ref[0, :, 0 : 2 * KV_W] = kv[:, 0 : 2 * KV_W]
    rows_ref[0, :, 2 * KV_W : 3 * KV_W] = _rope(kv[:, 2 * KV_W : 3 * KV_W], cos, sin)
    rows_ref[0, :, 3 * KV_W : 4 * KV_W] = kv[:, 3 * KV_W : 4 * KV_W]
    win_ref[0, :, 0:KV_W] = _rope(kv[:, 4 * KV_W : 5 * KV_W], cos, sin)
    win_ref[0, :, KV_W : 2 * KV_W] = kv[:, 5 * KV_W : 6 * KV_W]


def _kv_proj(x, kv_norm_g, w_kv, cos, sin, tm):
    bx, L, d = x.shape
    return pl.pallas_call(
        _kv_kernel,
        grid=(bx, L // tm),
        in_specs=[
            pl.BlockSpec((1, tm, d), lambda b, i: (b, i, 0)),
            pl.BlockSpec((1, d), lambda b, i: (0, 0)),
            pl.BlockSpec((d, 6 * KV_W), lambda b, i: (0, 0)),
            pl.BlockSpec((tm, KV_W), lambda b, i: (i, 0)),
            pl.BlockSpec((tm, KV_W), lambda b, i: (i, 0)),
        ],
        out_specs=[
            pl.BlockSpec((1, tm, 4 * KV_W), lambda b, i: (b, i, 0)),
            pl.BlockSpec((1, tm, 2 * KV_W), lambda b, i: (b, i, 0)),
        ],
        out_shape=[
            jax.ShapeDtypeStruct((bx, L, 4 * KV_W), F32),
            jax.ShapeDtypeStruct((bx, L, 2 * KV_W), F32),
        ],
        compiler_params=_cp(("parallel", "parallel")),
        name="kv_proj",
    )(x, kv_norm_g.reshape(1, d), w_kv, cos, sin)


def _qg_kernel(x_ref, mod_ref, g_ref, wq_ref, wg_ref, cos_ref, sin_ref, qn_ref, qr_ref, gate_ref):
    h = _pre(x_ref[0], g_ref, mod_ref[...], 1).astype(BF)
    q = _dot(h, wq_ref[...])
    gate_ref[0] = jax.nn.sigmoid(_dot(h, wg_ref[...]))
    scale = HEAD_DIM**-0.5 * LOG2E
    qn_ref[0] = (q * scale).astype(BF)
    cos = cos_ref[...]
    sin = sin_ref[...]
    for c in range(q.shape[1] // KV_W):
        sl = slice(c * KV_W, (c + 1) * KV_W)
        qr_ref[0, :, sl] = (_rope(q[:, sl], cos, sin) * scale).astype(BF)


def _qg_proj(x, mod, norm_g, wq, wg, cos, sin, layer, tm):
    bx, L, d = x.shape
    nq = wq.shape[-1]
    x_spec, mod_spec, g_spec = _row_specs(bx, tm, d, mod, layer, 1)
    return pl.pallas_call(
        _qg_kernel,
        grid=(bx, L // tm),
        in_specs=[
            x_spec,
            mod_spec,
            g_spec,
            pl.BlockSpec((d, nq), lambda b, i: (0, 0)),
            pl.BlockSpec((d, 128), lambda b, i: (0, 0)),
            pl.BlockSpec((tm, KV_W), lambda b, i: (i, 0)),
            pl.BlockSpec((tm, KV_W), lambda b, i: (i, 0)),
        ],
        out_specs=[
            pl.BlockSpec((1, tm, nq), lambda b, i: (b, i, 0)),
            pl.BlockSpec((1, tm, nq), lambda b, i: (b, i, 0)),
            pl.BlockSpec((1, tm, 128), lambda b, i: (b, i, 0)),
        ],
        out_shape=[
            jax.ShapeDtypeStruct((bx, L, nq), BF),
            jax.ShapeDtypeStruct((bx, L, nq), BF),
            jax.ShapeDtypeStruct((bx, L, 128), F32),
        ],
        compiler_params=_cp(("parallel", "parallel")),
        name="qg_proj",
    )(x, mod, norm_g, wq, wg, cos, sin)


def _op_kernel(o_ref, x_ref, mod_ref, g_ref, w_ref, y_ref):
    a = _dot(o_ref[0], w_ref[...])
    y_ref[0] = _post(x_ref[0], a, g_ref, mod_ref[...], 1, 1.0)


def _out_proj(o, x, mod, norm_g, w_o, layer, tm):
    bx, L, d = x.shape
    x_spec, mod_spec, g_spec = _row_specs(bx, tm, d, mod, layer, 1)
    return pl.pallas_call(
        _op_kernel,
        grid=(bx, L // tm),
        in_specs=[
            pl.BlockSpec((1, tm, o.shape[-1]), lambda b, i: (b, i, 0)),
            x_spec,
            mod_spec,
            g_spec,
            pl.BlockSpec(w_o.shape, lambda b, i: (0, 0)),
        ],
        out_specs=pl.BlockSpec((1, tm, d), lambda b, i: (b, i, 0)),
        out_shape=jax.ShapeDtypeStruct((bx, L, d), F32),
        compiler_params=_cp(("parallel", "parallel")),
        name="out_proj",
    )(o, x, mod, norm_g, w_o)


def _compress_tile(get_x, pe_ref, w1_ref, w2_ref, b_scr, m_rows):
    half = CMP_LEN // 2
    xs = [get_x(l) for l in range(half)]
    xa = jnp.concatenate([(xs[l] + pe_ref[l]).astype(BF) for l in range(half)], axis=1)
    xb = jnp.concatenate([(xs[l] + pe_ref[half + l]).astype(BF) for l in range(half)], axis=1)
    a = _dot(xa, w1_ref[0])
    b_scr[0:m_rows, :] = _dot(xb, w1_ref[1])
    b_scr[m_rows : m_rows + 8, :] = jnp.zeros((8, b_scr.shape[1]), F32)
    hid = a + b_scr[1 : m_rows + 1, :]
    return _dot(_silu(hid).astype(BF), w2_ref[...])


def _cmp_prompt_kernel(x_ref, pe_ref, w1_ref, w2_ref, o_ref, b_scr, *, ng, nc):
    row = lax.broadcasted_iota(jnp.int32, (ng, 128), 0)
    res = _compress_tile(
        lambda l: x_ref[0, pl.ds(l, ng, stride=CMP_STRIDE), :], pe_ref, w1_ref, w2_ref, b_scr, ng
    )
    o_ref[0] = jnp.where(row < nc, res, 0.0)


def _cmp_weights(cmp_pe, cmp_w1, cmp_w2):
    z1 = jnp.zeros_like(cmp_w1)
    w1bd = jnp.concatenate(
        [jnp.concatenate([cmp_w1, z1], axis=3), jnp.concatenate([z1, cmp_w1], axis=3)], axis=2
    )
    hid2 = w1bd.shape[-1]
    w1bd = w1bd.reshape(2, 2, (CMP_LEN // 2) * 2 * HEAD_DIM, hid2).astype(BF)
    z2 = jnp.zeros_like(cmp_w2)
    w2bd = jnp.concatenate(
        [jnp.concatenate([cmp_w2, z2], axis=2), jnp.concatenate([z2, cmp_w2], axis=2)], axis=1
    ).astype(BF)
    pe2 = jnp.concatenate([cmp_pe, cmp_pe], axis=2)[:, :, None, :]
    return pe2, w1bd, w2bd


def _compress_prompt(rows, pe2, w1bd, w2bd):
    bx, L, _ = rows.shape
    ng = L // CMP_STRIDE
    nc = (L - CMP_LEN) // CMP_STRIDE + 1
    kdim, hid2 = w1bd.shape[2], w1bd.shape[3]
    return pl.pallas_call(
        functools.partial(_cmp_prompt_kernel, ng=ng, nc=nc),
        grid=(4, bx),
        in_specs=[
            pl.BlockSpec((1, L, 128), lambda cg, b: (b, 0, cg)),
            pl.BlockSpec((None, CMP_LEN, 1, 128), lambda cg, b: (cg // 2, 0, 0, 0)),
            pl.BlockSpec((None, 2, kdim, hid2), lambda cg, b: (cg // 2, 0, 0, 0)),
            pl.BlockSpec((None, hid2, 128), lambda cg, b: (cg // 2, 0, 0)),
        ],
        out_specs=pl.BlockSpec((1, ng, 128), lambda cg, b: (b, 0, cg)),
        out_shape=jax.ShapeDtypeStruct((bx, ng, 2 * KV_W), F32),
        scratch_shapes=[pltpu.VMEM((ng + 8, hid2), F32)],
        compiler_params=_cp(("parallel", "parallel")),
        name="compress_prompt",
    )(rows, pe2, w1bd, w2bd)


def _cmp_sample_kernel(pt_ref, *refs, n_pages, sb, gpp, nc):
    del pt_ref
    page_refs = refs[: sb * n_pages]
    pe_ref, w1_ref, w2_ref, o_ref, b_scr = refs[sb * n_pages :]
    ng = n_pages * gpp
    m_rows = sb * ng
    row = lax.broadcasted_iota(jnp.int32, (ng, 128), 0)

    def get_x(l):
        return jnp.concatenate([r[0, pl.ds(l, gpp, stride=CMP_STRIDE), :] for r in page_refs], axis=0)

    res = _compress_tile(get_x, pe_ref, w1_ref, w2_ref, b_scr, m_rows)
    for s in range(sb):
        o_ref[s] = jnp.where(row < nc, res[s * ng : (s + 1) * ng, :], 0.0)


def _compress_sample(cache2, page_table, pe2, w1bd, w2bd, total_len, sb):
    db, n_pages = page_table.shape
    page = cache2.shape[1]
    gpp = page // CMP_STRIDE
    ng = n_pages * gpp
    nc = (total_len - CMP_LEN) // CMP_STRIDE + 1
    assert nc <= ng
    kdim, hid2 = w1bd.shape[2], w1bd.shape[3]

    def page_spec(s, p):
        return pl.BlockSpec((1, page, 128), lambda cg, bb, pt: (pt[bb * sb + s, p], 0, cg))

    grid_spec = pltpu.PrefetchScalarGridSpec(
        num_scalar_prefetch=1,
        grid=(4, db // sb),
        in_specs=[page_spec(s, p) for s in range(sb) for p in range(n_pages)]
        + [
            pl.BlockSpec((None, CMP_LEN, 1, 128), lambda cg, bb, pt: (cg // 2, 0, 0, 0)),
            pl.BlockSpec((None, 2, kdim, hid2), lambda cg, bb, pt: (cg // 2, 0, 0, 0)),
            pl.BlockSpec((None, hid2, 128), lambda cg, bb, pt: (cg // 2, 0, 0)),
        ],
        out_specs=pl.BlockSpec((sb, ng, 128), lambda cg, bb, pt: (bb, 0, cg)),
        scratch_shapes=[pltpu.VMEM((sb * ng + 8, hid2), F32)],
    )
    return pl.pallas_call(
        functools.partial(_cmp_sample_kernel, n_pages=n_pages, sb=sb, gpp=gpp, nc=nc),
        grid_spec=grid_spec,
        out_shape=jax.ShapeDtypeStruct((db, ng, 2 * KV_W), F32),
        compiler_params=_cp(("parallel", "parallel")),
        name="compress_sample",
    )(page_table, *([cache2] * (sb * n_pages)), pe2, w1bd, w2bd)


def _select_scores(imp, blk, pos):
    cur = pos // SLC_BLOCK
    valid = blk * SLC_BLOCK <= pos
    forced = (blk == 0) | (blk == cur) | (blk == cur - 1)
    return jnp.where(valid, imp + FORCE_BONUS * forced.astype(F32), -1.0)


def _topk_rows(score, k):
    n = score.shape[0]
    ridx = lax.broadcasted_iota(jnp.int32, score.shape, 0).astype(F32)

    def body(_, sc):
        m = jnp.max(sc, axis=0, keepdims=True)
        first = jnp.min(jnp.where(sc == m, ridx, float(n)), axis=0, keepdims=True)
        return jnp.where(ridx == first, -jnp.inf, sc)

    sc = lax.fori_loop(0, k, body, score, unroll=True)
    return (sc == -jnp.inf) & (score >= 0.0)


BIAS_ROWS = 16


def _masked_attend(k_rows, q, bias, v_aug):
    s = _dot(k_rows, q) + jnp.concatenate([bias] * GROUP, axis=1)
    e = jnp.exp2(s - jnp.max(s, axis=0, keepdims=True))
    return e, _dot(v_aug, e.astype(BF))


def _attn_prompt_kernel(
    qn_ref, qr_ref, g_ref, kc_ref, vct_ref, ks_ref, vst_ref, kw_ref, vwt_ref, ovt_ref, ctab_ref, wtab_ref,
    o_ref, bias_scr, qaug_scr, sa_scr, sb_scr, m_scr, acc_scr, *, tq, tk, ns, top_n,
):
    start = pl.program_id(2) * tq
    gq = GROUP * tq
    bpc = tk // SLC_BLOCK

    def heads_on_lanes(ref):
        t = ref[0]
        return jnp.concatenate([t[HEAD_DIM * g : HEAD_DIM * (g + 1), :] for g in range(GROUP)], axis=1)

    qn = heads_on_lanes(qn_ref)
    qr = heads_on_lanes(qr_ref)
    pos = start + lax.broadcasted_iota(jnp.int32, (1, tq), 1)
    pos_g = jnp.concatenate([pos] * GROUP, axis=1)

    ngp = kc_ref.shape[2]
    c0 = pl.multiple_of(ngp - start // CMP_STRIDE, 8)
    e_c, un_c = _masked_attend(kc_ref[0, 0], qn, ctab_ref[pl.ds(c0, ngp), :], vct_ref[0, 0])
    inv_c = jnp.where(pos_g >= CMP_LEN - 1, 1.0 / jnp.maximum(un_c[HEAD_DIM : HEAD_DIM + 1, :], 1e-30), 0.0)
    o_c = un_c[0:HEAD_DIM, :] * inv_c
    psum = e_c[:, 0:tq] * inv_c[:, 0:tq]
    for g in range(1, GROUP):
        psum = psum + e_c[:, g * tq : (g + 1) * tq] * inv_c[:, g * tq : (g + 1) * tq]
    imp = _split_dot(ovt_ref[...], psum)

    blk = lax.broadcasted_iota(jnp.int32, (NS_PAD, 1), 0)
    score = jnp.where(blk < ns, _select_scores(imp, blk, pos), -jnp.inf)
    sel_bias = jnp.where(_topk_rows(score, top_n), 0.0, NEG)
    for c in range(NS_PAD // bpc):
        bias_scr[BIAS_ROWS * c : BIAS_ROWS * c + bpc, :] = sel_bias[bpc * c : bpc * (c + 1), :]
        if bpc < BIAS_ROWS:
            bias_scr[BIAS_ROWS * c + bpc : BIAS_ROWS * (c + 1), :] = jnp.zeros((BIAS_ROWS - bpc, tq), F32)

    wk = WINDOW + tq
    w0 = pl.multiple_of(jnp.maximum(start - WINDOW, 0), 128)
    t0 = pl.multiple_of(WINDOW - (start - w0), 128)
    _, un_w = _masked_attend(
        kw_ref[0, 0, pl.ds(w0, wk), :], qr, wtab_ref[pl.ds(t0, wk), :], vwt_ref[0, 0, :, pl.ds(w0, wk)]
    )
    o_w = un_w[0:HEAD_DIM, :] * (1.0 / jnp.maximum(un_w[HEAD_DIM : HEAD_DIM + 1, :], 1e-30))

    qaug_scr[0:HEAD_DIM, :] = qr
    qaug_scr[HEAD_DIM:, :] = jnp.zeros((qaug_scr.shape[0] - HEAD_DIM, gq), BF)
    m_scr[...] = jnp.full(m_scr.shape, NEG, F32)
    acc_scr[...] = jnp.zeros(acc_scr.shape, F32)

    def scores(c, s_ref):
        k0 = pl.multiple_of(c * tk, tk)
        b = bias_scr[pl.ds(pl.multiple_of(c * BIAS_ROWS, BIAS_ROWS), BIAS_ROWS), :]
        qaug_scr[HEAD_DIM : HEAD_DIM + BIAS_ROWS, :] = jnp.concatenate([b] * GROUP, axis=1).astype(BF)
        s_ref[...] = _dot(ks_ref[0, 0, pl.ds(k0, tk), :], qaug_scr[...])

    def accumulate(c, s_ref, causal):
        k0 = pl.multiple_of(c * tk, tk)
        s = s_ref[...]
        if causal:
            kpos = k0 + lax.broadcasted_iota(jnp.int32, (tk, 1), 0)
            s = s + jnp.concatenate([jnp.where(kpos <= pos, 0.0, NEG)] * GROUP, axis=1)
        m_old = m_scr[...]
        m_new = jnp.maximum(m_old, jnp.max(s, axis=0, keepdims=True))
        p = jnp.exp2(s - m_new).astype(BF)
        acc_scr[...] = jnp.exp2(m_old - m_new) * acc_scr[...] + _dot(vst_ref[0, 0, :, pl.ds(k0, tk)], p)
        m_scr[...] = m_new

    def pair(j, carry):
        scores(2 * j + 1, sb_scr)
        accumulate(2 * j, sa_scr, False)
        scores(2 * j + 2, sa_scr)
        accumulate(2 * j + 1, sb_scr, False)
        return carry

    last = (start + tq - 1) // tk
    scores(0, sa_scr)
    lax.fori_loop(0, last // 2, pair, 0)

    @pl.when(last % 2 == 0)
    def _():
        accumulate(last, sa_scr, True)

    @pl.when(last % 2 == 1)
    def _():
        scores(last, sb_scr)
        accumulate(last - 1, sa_scr, False)
        accumulate(last, sb_scr, True)

    acc = acc_scr[...]
    o_s = acc[0:HEAD_DIM, :] * (1.0 / jnp.maximum(acc[HEAD_DIM : HEAD_DIM + 1, :], 1e-30))

    gates = g_ref[0, 0]
    outs = []
    for g in range(GROUP):
        sl = slice(g * tq, (g + 1) * tq)
        outs.append(
            gates[3 * g : 3 * g + 1, :] * o_c[:, sl]
            + gates[3 * g + 1 : 3 * g + 2, :] * o_s[:, sl]
            + gates[3 * g + 2 : 3 * g + 3, :] * o_w[:, sl]
        )
    o_ref[0] = jnp.concatenate(outs, axis=0).astype(BF)


def _attn_prompt(qn_t, qr_t, gates_t, kc, vc_t, ks, vs_t, kw, vw_t, ov_t, ctab, wtab, ns, tq, tk):
    bx, _, L = qn_t.shape
    ngp = kc.shape[2]
    gd = GROUP * HEAD_DIM
    kaug = ks.shape[-1]
    vaug = vs_t.shape[2]
    kv_spec = pl.BlockSpec((1, 1, L, HEAD_DIM), lambda b, k, i: (b, k, 0, 0))
    vt_spec = pl.BlockSpec((1, 1, vaug, L), lambda b, k, i: (b, k, 0, 0))
    q_spec = pl.BlockSpec((1, gd, tq), lambda b, k, i: (b, k, i))
    return pl.pallas_call(
        functools.partial(_attn_prompt_kernel, tq=tq, tk=tk, ns=ns, top_n=min(TOP_N, ns)),
        grid=(bx, N_KV_HEADS, L // tq),
        in_specs=[
            q_spec,
            q_spec,
            pl.BlockSpec((1, 1, 16, tq), lambda b, k, i: (b, k, 0, i)),
            pl.BlockSpec((1, 1, ngp, HEAD_DIM), lambda b, k, i: (b, k, 0, 0)),
            pl.BlockSpec((1, 1, vaug, ngp), lambda b, k, i: (b, k, 0, 0)),
            pl.BlockSpec((1, 1, L, kaug), lambda b, k, i: (b, k, 0, 0)),
            vt_spec,
            kv_spec,
            vt_spec,
            pl.BlockSpec(ov_t.shape, lambda b, k, i: (0, 0)),
            pl.BlockSpec(ctab.shape, lambda b, k, i: (0, 0)),
            pl.BlockSpec(wtab.shape, lambda b, k, i: (0, 0)),
        ],
        out_specs=q_spec,
        out_shape=jax.ShapeDtypeStruct(qn_t.shape, BF),
        scratch_shapes=[
            pltpu.VMEM((NS_PAD // (tk // SLC_BLOCK) * BIAS_ROWS, tq), F32),
            pltpu.VMEM((kaug, GROUP * tq), BF),
            pltpu.VMEM((tk, GROUP * tq), F32),
            pltpu.VMEM((tk, GROUP * tq), F32),
            pltpu.VMEM((1, GROUP * tq), F32),
            pltpu.VMEM((vaug, GROUP * tq), F32),
        ],
        compiler_params=_cp(("parallel", "parallel", "arbitrary")),
        name="attn_prompt",
    )(qn_t, qr_t, gates_t, kc, vc_t, ks, vs_t, kw, vw_t, ov_t, ctab, wtab)


def _softmax_rows_ext(s, mask, s_new, m_new_mask):
    sm = jnp.where(mask, s, NEG)
    sn = jnp.where(m_new_mask, s_new, NEG)
    mx = jnp.maximum(jnp.max(sm, axis=1, keepdims=True), sn)
    e = jnp.exp2(sm - mx) * mask.astype(F32)
    en = jnp.exp2(sn - mx) * m_new_mask.astype(F32)
    inv = 1.0 / jnp.maximum(jnp.sum(e, axis=1, keepdims=True) + en, 1e-30)
    return e * inv, en * inv


def _attn_sample_kernel(pt_ref, *refs, n_pages, sb, **static):
    del pt_ref
    rest = refs[sb * n_pages :]
    stages = [_attn_sample_stages(s, refs[s * n_pages : (s + 1) * n_pages], *rest, **static) for s in range(sb)]
    for _ in range(4):
        for st in stages:
            next(st)


def _attn_sample_stages(
    s_idx, page_refs, qn_ref, qr_ref, g_ref, cmp_ref, new_ref, win_ref, wnew_ref, ov_ref, exp_ref, gsum_ref, o_ref,
    *, page, past_len, nc, ns, top_n, win_buf,
):
    pos = past_len
    qn = qn_ref[s_idx]
    qr = qr_ref[s_idx]
    qr32 = qr.astype(F32)
    nh = qn.shape[0]

    cmp = cmp_ref[s_idx]
    kc = cmp[:, 0:KV_W].astype(BF)
    vc = cmp[:, KV_W : 2 * KV_W].astype(BF)
    ngp = kc.shape[0]
    new = new_ref[s_idx]
    k_new = new[:, 2 * KV_W : 3 * KV_W]
    v_new = new[:, 3 * KV_W : 4 * KV_W]
    wbuf = win_ref[s_idx]
    wnew = wnew_ref[s_idx]
    s_c = _dot_nt(qn, kc)
    s_w = _dot_nt(qr, wbuf[:, 0:KV_W].astype(BF))
    s_wn = jnp.sum(qr32 * wnew[:, 0:KV_W].astype(BF).astype(F32), axis=1, keepdims=True)
    s = jnp.concatenate([_dot_nt(qr, r[0][:, 0:KV_W].astype(BF)) for r in page_refs], axis=1)
    s_new = jnp.sum(qr32 * k_new.astype(BF).astype(F32), axis=1, keepdims=True)
    yield

    n_idx = lax.broadcasted_iota(jnp.int32, (1, ngp), 1)
    cmask = jnp.broadcast_to((n_idx * CMP_STRIDE + (CMP_LEN - 1) <= pos) & (n_idx < nc), (nh, ngp))
    sm = jnp.where(cmask, s_c, NEG)
    e = jnp.exp2(sm - jnp.max(sm, axis=1, keepdims=True)) * cmask.astype(F32)
    p_c = e / jnp.maximum(jnp.sum(e, axis=1, keepdims=True), 1e-30)
    o_c = _dot(p_c.astype(BF), vc)
    t = _split_dot_r(p_c, ov_ref[...])
    imp = _split_dot(gsum_ref[...], t)
    yield

    blk = lax.broadcasted_iota(jnp.int32, (1, NS_PAD), 1)
    score = jnp.where(blk < ns, _select_scores(imp, blk, pos), -jnp.inf)
    pad = jnp.concatenate([score, jnp.zeros((NS_PAD - nh, NS_PAD), F32)], axis=0)
    score_t = pad.T
    ii = lax.broadcasted_iota(jnp.int32, (NS_PAD, NS_PAD), 0)
    jj = lax.broadcasted_iota(jnp.int32, (NS_PAD, NS_PAD), 1)
    head_grp = lax.broadcasted_iota(jnp.int32, (nh, NS_PAD), 0) // GROUP
    sel = jnp.zeros((nh, NS_PAD), F32)
    for k in range(nh // GROUP):
        h = k * GROUP
        a = score_t[:, h : h + 1]
        b = score[h : h + 1, :]
        beats = (a > b) | ((a == b) & (ii < jj))
        rank = jnp.sum(beats.astype(F32), axis=0, keepdims=True)
        sel_k = jnp.where((rank < float(top_n)) & (b >= 0.0), 1.0, 0.0)
        sel = jnp.where(head_grp == k, sel_k, sel)

    kmask = _dot(sel.astype(BF), exp_ref[...]) > 0.5
    nb = pos // SLC_BLOCK
    new_sel = sel[:, nb : nb + 1] > 0.5
    yield

    kw_pos = (past_len - win_buf) + lax.broadcasted_iota(jnp.int32, (1, win_buf), 1)
    rel = pos - kw_pos
    wmask = jnp.broadcast_to((rel >= 0) & (rel < WINDOW) & (kw_pos >= 0), (nh, win_buf))
    p_w, p_wn = _softmax_rows_ext(s_w, wmask, s_wn, jnp.full((nh, 1), True))
    o_w = _dot(p_w.astype(BF), wbuf[:, KV_W : 2 * KV_W].astype(BF)) + p_wn * wnew[:, KV_W : 2 * KV_W].astype(BF).astype(F32)
    p_s, p_new = _softmax_rows_ext(s, kmask, s_new, new_sel)
    o_s = p_new * v_new.astype(BF).astype(F32)
    for i, r in enumerate(page_refs):
        o_s = o_s + _dot(p_s[:, i * page : (i + 1) * page].astype(BF), r[0][:, KV_W : 2 * KV_W].astype(BF))
    gates = g_ref[s_idx]
    o_ref[s_idx] = gates[:, 0:1] * o_c + gates[:, 1:2] * o_s + gates[:, 2:3] * o_w
    yield


def _split_dot_r(a_f32, b):
    hi = a_f32.astype(BF)
    lo = (a_f32 - hi.astype(F32)).astype(BF)
    return _dot(hi, b) + _dot(lo, b)


def _attn_sample(cache2, page_table, qn_bd, qr_bd, gates, cmp_s, rows_new, win_state, win_new, ov, expand, gsum, nc, ns):
    db, n_pages = page_table.shape
    page = cache2.shape[1]
    past_len = n_pages * page
    win_buf = win_state.shape[1]
    ngp = cmp_s.shape[1]
    sb = _pick_tile(db, 2)

    def one(shape):
        return pl.BlockSpec((sb,) + shape, lambda b, pt: (b, 0, 0))

    def const(a):
        return pl.BlockSpec(a.shape, lambda b, pt: (0, 0))

    def page_spec(s, p):
        return pl.BlockSpec((1, page, 2 * KV_W), lambda b, pt: (pt[b * sb + s, p], 0, 1))

    grid_spec = pltpu.PrefetchScalarGridSpec(
        num_scalar_prefetch=1,
        grid=(db // sb,),
        in_specs=[page_spec(s, p) for s in range(sb) for p in range(n_pages)]
        + [
            one((N_HEADS, KV_W)),
            one((N_HEADS, KV_W)),
            one((N_HEADS, 3)),
            one((ngp, 2 * KV_W)),
            one((1, 4 * KV_W)),
            one((win_buf, 2 * KV_W)),
            one((1, 2 * KV_W)),
            const(ov),
            const(expand),
            const(gsum),
        ],
        out_specs=one((N_HEADS, KV_W)),
    )
    return pl.pallas_call(
        functools.partial(
            _attn_sample_kernel, n_pages=n_pages, sb=sb, page=page, past_len=past_len, nc=nc, ns=ns,
            top_n=min(TOP_N, ns), win_buf=win_buf,
        ),
        grid_spec=grid_spec,
        out_shape=jax.ShapeDtypeStruct((db, N_HEADS, KV_W), F32),
        compiler_params=_cp(("parallel",)),
        name="attn_sample",
    )(page_table, *([cache2] * (sb * n_pages)), qn_bd, qr_bd, gates, cmp_s, rows_new, win_state, win_new, ov, expand, gsum)


def _rope_tables(pos):
    half = HEAD_DIM // 2
    inv = ROPE_THETA ** (-jnp.arange(half, dtype=F32) / half)
    ang = pos.astype(F32)[:, None] * inv[None]
    cos, sin = jnp.cos(ang), jnp.sin(ang)
    cos = jnp.tile(jnp.concatenate([cos, cos], axis=1), (1, N_KV_HEADS))
    sin = jnp.tile(jnp.concatenate([-sin, sin], axis=1), (1, N_KV_HEADS))
    return cos, sin


def _overlap(n_rows, n_cols):
    cs = jnp.arange(n_rows)[:, None] * CMP_STRIDE
    ss = jnp.arange(n_cols)[None] * SLC_BLOCK
    return ((cs < ss + SLC_BLOCK) & (cs + CMP_LEN > ss)).astype(F32)


def _pick_tile(n, pref):
    t = min(n, pref)
    while n % t:
        t //= 2
    return t


def kernel(x_prompt, x_sample, cache_kv, state_win_kv, state_conv, page_table, c_prompt, c_sample,
           ada_w, ada_b, norm_g, ffn_w_in, ffn_w_out, conv_w_in, conv_k, conv_w_out,
           kv_norm_g, w_kv, cmp_pe, cmp_w1, cmp_w2, nsa_w_qg, nsa_w_o):
    bp, seq, d = x_prompt.shape
    db, dec_seq, _ = x_sample.shape
    assert dec_seq == 1
    depth = ada_w.shape[0]
    n_a = conv_w_in.shape[0]
    n_pages = page_table.shape[1]
    page = cache_kv.shape[1]
    past_len = n_pages * page
    win_buf = state_win_kv.shape[1]
    dff = ffn_w_out.shape[2]
    nq = N_HEADS * HEAD_DIM

    ffn_w_in_b = ffn_w_in.astype(BF)
    ffn_w_out_b = ffn_w_out.astype(BF)
    conv_w_in_b = conv_w_in.astype(BF)
    conv_w_out_b = conv_w_out.astype(BF)
    w_kv_b = w_kv.astype(BF)
    wq_b = nsa_w_qg[:, :, :nq].astype(BF)
    wg_b = jnp.pad(nsa_w_qg[:, :, nq:], ((0, 0), (0, 0), (0, 128 - 3 * N_HEADS))).astype(BF)
    w_o_b = nsa_w_o.astype(BF)
    pe2, w1bd, w2bd = _cmp_weights(cmp_pe, cmp_w1, cmp_w2)

    r_pad = -(db + bp) % 8
    c_all = jnp.concatenate([c_sample, c_prompt, jnp.zeros((r_pad, d), F32)], axis=0)
    mod_all = _ada_mod(c_all, ada_w, ada_b)
    mod_s = mod_all[:, :, None, :db, :]
    mod_p = mod_all[:, :, db : db + bp, None, :]

    def trunk(x, mod, tm, conv_prev, seq_mode, build_kv, attend):
        conv_states = []
        parts = 2 if tm % 512 == 0 else 1
        for l in range(depth):
            x = _ffn_block(x, mod, norm_g, ffn_w_in_b, ffn_w_out_b, l, 0, 0, tm, parts)
            if l < n_a:
                p0, p1 = conv_prev(l)
                x, st = _conv_block(x, mod, norm_g, conv_w_in_b, conv_k, conv_w_out_b, p0, p1, l, tm, seq_mode)
                conv_states.append(st)
            else:
                j = l - n_a
                x = attend(x, mod, l, j)
            x = _ffn_block(x, mod, norm_g, ffn_w_in_b, ffn_w_out_b, l, 2, 1, tm, parts)
            if l == n_a - 1:
                build_kv(x)
        return x, conv_states

    tm_p = _pick_tile(seq, 512)
    cos_p, sin_p = _rope_tables(jnp.arange(seq, dtype=jnp.int32))
    ng_p = seq // CMP_STRIDE
    ns_p = -(-seq // SLC_BLOCK)
    assert ns_p <= NS_PAD and seq % Q_BLOCK == 0 and seq >= WINDOW + Q_BLOCK
    ov_t_p = _overlap(ng_p, NS_PAD).T.astype(BF)
    tk_p = _pick_tile(seq, BIAS_ROWS * SLC_BLOCK)
    qoff = jnp.arange(Q_BLOCK)[None, :]
    cu = jnp.arange(2 * ng_p)[:, None] - ng_p
    ctab_p = jnp.where(cu * CMP_STRIDE + (CMP_LEN - 1) <= qoff, 0.0, NEG).astype(F32)
    wu = jnp.arange(2 * WINDOW + Q_BLOCK)[:, None]
    wtab_p = jnp.where((wu > qoff) & (wu <= qoff + WINDOW), 0.0, NEG).astype(F32)
    ctx_p = {}

    def build_prompt(x):
        rows, win = _kv_proj(x, kv_norm_g, w_kv_b, cos_p, sin_p, tm_p)
        cmp = _compress_prompt(rows, pe2, w1bd, w2bd)

        def heads_major(a):
            return a.reshape(bp, a.shape[1], N_KV_HEADS, HEAD_DIM).transpose(0, 2, 1, 3).astype(BF)

        def values_t(a):
            n = a.shape[1]
            ones_rows = jnp.broadcast_to((jnp.arange(BIAS_ROWS)[:, None] == 0).astype(BF), (BIAS_ROWS, n))
            return jnp.concatenate(
                [
                    a.transpose(0, 2, 1).astype(BF).reshape(bp, N_KV_HEADS, HEAD_DIM, n),
                    jnp.broadcast_to(ones_rows, (bp, N_KV_HEADS, BIAS_ROWS, n)),
                ],
                axis=2,
            )

        blk_in_chunk = (jnp.arange(seq) % tk_p) // SLC_BLOCK
        onehot = (blk_in_chunk[:, None] == jnp.arange(128 - HEAD_DIM)[None, :]).astype(BF)
        ks = jnp.concatenate(
            [heads_major(rows[:, :, 2 * KV_W : 3 * KV_W]), jnp.broadcast_to(onehot, (bp, N_KV_HEADS) + onehot.shape)],
            axis=3,
        )
        ctx_p.update(
            rows=rows, win=win,
            kc=heads_major(cmp[:, :, :KV_W]), vc_t=values_t(cmp[:, :, KV_W:]),
            ks=ks, vs_t=values_t(rows[:, :, 3 * KV_W :]),
            kw=heads_major(win[:, :, :KV_W]), vw_t=values_t(win[:, :, KV_W:]),
        )

    def attend_prompt(x, mod, l, j):
        qn, qr, gates = _qg_proj(x, mod, norm_g, wq_b[j], wg_b[j], cos_p, sin_p, l, tm_p)
        gates_t = jnp.pad(
            gates[:, :, : 3 * N_HEADS].reshape(bp, seq, N_KV_HEADS, 3 * GROUP), ((0, 0), (0, 0), (0, 0), (0, 16 - 3 * GROUP))
        ).transpose(0, 2, 3, 1)
        o_t = _attn_prompt(
            qn.transpose(0, 2, 1), qr.transpose(0, 2, 1), gates_t, ctx_p["kc"], ctx_p["vc_t"], ctx_p["ks"],
            ctx_p["vs_t"], ctx_p["kw"], ctx_p["vw_t"], ov_t_p, ctab_p, wtab_p, ns_p, Q_BLOCK, tk_p,
        )
        return _out_proj(o_t.transpose(0, 2, 1), x, mod, norm_g, w_o_b[j], l, tm_p)

    zero_row = jnp.zeros((bp, 1, d), F32)
    y_prompt, conv_p = trunk(x_prompt, mod_p, tm_p, lambda l: (zero_row, zero_row), True, build_prompt, attend_prompt)
    kv_rows_p = ctx_p["rows"].reshape(bp, seq, 4, N_KV_HEADS, HEAD_DIM)
    wbp = min(WINDOW, seq)
    win_p = ctx_p["win"][:, seq - wbp :].reshape(bp, wbp, 2, N_KV_HEADS, HEAD_DIM)
    conv_p = jnp.stack(conv_p)

    xs = x_sample.reshape(1, db, d)
    tm_s = db
    cos_s, sin_s = _rope_tables(jnp.full((db,), past_len, jnp.int32))
    total = past_len + 1
    nc_s = (total - CMP_LEN) // CMP_STRIDE + 1
    ns_s = -(-total // SLC_BLOCK)
    ng_s = past_len // CMP_STRIDE
    assert ns_s <= NS_PAD and ng_s <= 128 and past_len // SLC_BLOCK < ns_s
    cache2 = cache_kv.reshape(cache_kv.shape[0], page, 4 * KV_W)
    win_state = state_win_kv.reshape(db, win_buf, 2 * KV_W)
    ov_s = _overlap(ng_s, NS_PAD).astype(BF)
    expand = (jnp.arange(past_len)[None, :] // SLC_BLOCK == jnp.arange(NS_PAD)[:, None]).astype(BF)
    head_kv = jnp.arange(N_HEADS) // GROUP
    gsum = (head_kv[:, None] == head_kv[None, :]).astype(BF)
    bd_mask = (head_kv[:, None] == jnp.arange(N_KV_HEADS)[None, :]).astype(BF)
    ctx_s = {}

    def build_sample(x):
        rows, win = _kv_proj(x, kv_norm_g, w_kv_b, cos_s, sin_s, tm_s)
        ctx_s.update(
            rows=rows.reshape(db, 1, 4 * KV_W), win=win.reshape(db, 1, 2 * KV_W),
            cmp=_compress_sample(cache2, page_table, pe2, w1bd, w2bd, total, _pick_tile(db, 4)),
        )

    def block_diag(q):
        qh = q.reshape(db, N_HEADS, 1, HEAD_DIM) * bd_mask[None, :, :, None]
        return qh.reshape(db, N_HEADS, KV_W)

    def attend_sample(x, mod, l, j):
        qn, qr, gates = _qg_proj(x, mod, norm_g, wq_b[j], wg_b[j], cos_s, sin_s, l, tm_s)
        o_full = _attn_sample(
            cache2, page_table, block_diag(qn[0]), block_diag(qr[0]), gates[0, :, : 3 * N_HEADS].reshape(db, N_HEADS, 3),
            ctx_s["cmp"], ctx_s["rows"], win_state, ctx_s["win"], ov_s, expand, gsum, nc_s, ns_s,
        )
        o = jnp.take_along_axis(
            o_full.reshape(db, N_HEADS, N_KV_HEADS, HEAD_DIM), head_kv[None, :, None, None], axis=2
        ).reshape(1, db, nq)
        return _out_proj(o.astype(BF), x, mod, norm_g, w_o_b[j], l, tm_s)

    def conv_prev_s(l):
        return state_conv[l, :, 0, :].reshape(1, db, d), state_conv[l, :, 1, :].reshape(1, db, d)

    y_s, conv_u = trunk(xs, mod_s, tm_s, conv_prev_s, False, build_sample, attend_sample)
    y_sample = y_s.reshape(db, 1, d)
    kv_rows_s = ctx_s["rows"].reshape(db, 1, 4, N_KV_HEADS, HEAD_DIM)
    win_s = jnp.concatenate([state_win_kv, ctx_s["win"].reshape(db, 1, 2, N_KV_HEADS, HEAD_DIM)], axis=1)[:, 1:]
    conv_s = jnp.stack([jnp.stack([state_conv[l, :, 1, :], conv_u[l][0]], axis=1) for l in range(n_a)])

    return (y_prompt, y_sample, kv_rows_p, win_p, conv_p, kv_rows_s, win_s, conv_s)
```

```python
import functools

import jax
import jax.numpy as jnp
from jax import lax
from jax.experimental import pallas as pl
from jax.experimental.pallas import tpu as pltpu

F32 = jnp.float32
BF = jnp.bfloat16

HEAD_DIM = 64
N_HEADS = 16
N_KV_HEADS = 4
GROUP = N_HEADS // N_KV_HEADS
KV_W = N_KV_HEADS * HEAD_DIM
CMP_LEN = 32
CMP_STRIDE = 16
SLC_BLOCK = 64
TOP_N = 16
WINDOW = 512
Q_BLOCK = 256
CONV_W = 3
MACARON_W = 0.5
ROPE_THETA = 10000.0
EPS = 1e-6
FORCE_BONUS = 1e3
LOG2E = 1.4426950408889634
NEG = -1e30
N_MOD = 9
NS_PAD = 128
VMEM_LIMIT = 52 * 1024 * 1024


def _cp(sem):
    return pltpu.CompilerParams(dimension_semantics=sem, vmem_limit_bytes=VMEM_LIMIT)


def _dot(a, b):
    return jnp.dot(a, b, preferred_element_type=F32)


def _dot_nt(a, b):
    return lax.dot_general(a, b, (((1,), (1,)), ((), ())), preferred_element_type=F32)


def _split_dot(a, b_f32):
    hi = b_f32.astype(BF)
    lo = (b_f32 - hi.astype(F32)).astype(BF)
    return _dot(a, hi) + _dot(a, lo)


def _rms(x, g):
    return x * lax.rsqrt(jnp.mean(x * x, axis=-1, keepdims=True) + EPS) * g


def _silu(a):
    return a * jax.nn.sigmoid(a)


def _rope(x, cos, sin_signed):
    n = x.shape[-1]
    lane = lax.broadcasted_iota(jnp.int32, x.shape, 1)
    first = (lane & (HEAD_DIM // 2)) == 0
    partner = jnp.where(first, pltpu.roll(x, n - HEAD_DIM // 2, axis=1), pltpu.roll(x, HEAD_DIM // 2, axis=1))
    return x * cos + partner * sin_signed


def _ada_kernel(c_ref, w_ref, b_ref, o_ref):
    cond = _silu(c_ref[...]).astype(BF)
    o_ref[0, 0] = _dot(cond, w_ref[0].astype(BF)) + b_ref[0, 0]


def _ada_mod(c_all, ada_w, ada_b):
    depth, d, _ = ada_w.shape
    r = c_all.shape[0]
    return pl.pallas_call(
        _ada_kernel,
        grid=(depth, N_MOD),
        in_specs=[
            pl.BlockSpec((r, d), lambda l, j: (0, 0)),
            pl.BlockSpec((1, d, d), lambda l, j: (l, 0, j)),
            pl.BlockSpec((1, 1, 1, d), lambda l, j: (l, j, 0, 0)),
        ],
        out_specs=pl.BlockSpec((1, 1, r, d), lambda l, j: (l, j, 0, 0)),
        out_shape=jax.ShapeDtypeStruct((depth, N_MOD, r, d), F32),
        compiler_params=_cp(("parallel", "parallel")),
        name="ada_mod",
    )(c_all, ada_w, ada_b.reshape(depth, N_MOD, 1, d))


def _row_specs(bx, tm, d, mod, layer, sub):
    lm = mod.shape[3]
    tml = 1 if lm == 1 else tm
    x_spec = pl.BlockSpec((1, tm, d), lambda b, i, *_: (b, i, 0))
    mod_spec = pl.BlockSpec(
        (None, 3, None, tml, d), lambda b, i, *_: (layer, sub, b, 0 if lm == 1 else i, 0)
    )
    g_spec = pl.BlockSpec((None, 6, d), lambda b, i, *_: (layer, 0, 0))
    return x_spec, mod_spec, g_spec


def _pre(x, g_ref, m, sub):
    return _rms(x, g_ref[2 * sub : 2 * sub + 1, :]) * (1.0 + m[1]) + m[0]


def _post(x, o, g_ref, m, sub, w):
    return x + w * m[2] * _rms(o, g_ref[2 * sub + 1 : 2 * sub + 2, :])


def _ffn_kernel(x_ref, mod_ref, g_ref, wa_ref, wb_ref, wo_ref, y_ref, *, sub, parts):
    m = mod_ref[...]
    tm = x_ref.shape[1]
    rows = tm // parts
    for r in range(parts):
        sl = slice(r * rows, (r + 1) * rows)
        x = x_ref[0, sl, :]
        mr = m if m.shape[1] == 1 else m[:, sl, :]
        h = _pre(x, g_ref, mr, sub).astype(BF)
        u = _silu(_dot(h, wa_ref[...])) * _dot(h, wb_ref[...])
        o = _dot(u.astype(BF), wo_ref[...])
        y_ref[0, sl, :] = _post(x, o, g_ref, mr, sub, MACARON_W)


def _ffn_block(x, mod, norm_g, w_in, w_out, layer, sub, half, tm, parts):
    bx, L, d = x.shape
    dff = w_out.shape[2]
    x_spec, mod_spec, g_spec = _row_specs(bx, tm, d, mod, layer, sub)
    once = pl.Buffered(1)
    return pl.pallas_call(
        functools.partial(_ffn_kernel, sub=sub, parts=parts),
        grid=(bx, L // tm),
        in_specs=[
            x_spec,
            mod_spec,
            g_spec,
            pl.BlockSpec((None, None, d, dff), lambda b, i: (layer, half, 0, 0), pipeline_mode=once),
            pl.BlockSpec((None, None, d, dff), lambda b, i: (layer, half, 0, 1), pipeline_mode=once),
            pl.BlockSpec((None, None, dff, d), lambda b, i: (layer, half, 0, 0), pipeline_mode=once),
        ],
        out_specs=pl.BlockSpec((1, tm, d), lambda b, i: (b, i, 0)),
        out_shape=jax.ShapeDtypeStruct((bx, L, d), F32),
        compiler_params=_cp(("parallel", "parallel")),
        name="ffn_block",
    )(x, mod, norm_g, w_in, w_in, w_out)


def _conv_kernel(x_ref, mod_ref, g_ref, win_ref, ck_ref, wout_ref, p0_ref, p1_ref, y_ref, u_ref, u_scr, *, seq_mode, tm, parts):
    d = x_ref.shape[-1]
    m = mod_ref[...]
    ck = ck_ref[...]
    rows = tm // parts
    if seq_mode:
        @pl.when(pl.program_id(1) == 0)
        def _():
            u_scr[6:7, :] = p0_ref[0]
            u_scr[7:8, :] = p1_ref[0]

    for r in range(parts):
        sl = slice(r * rows, (r + 1) * rows)
        x = x_ref[0, sl, :]
        mr = m if m.shape[1] == 1 else m[:, sl, :]
        h = _pre(x, g_ref, mr, 1).astype(BF)
        z = _dot(h, win_ref[...])
        bg = z[:, 0:d]
        u = z[:, d : 2 * d] * z[:, 2 * d : 3 * d]
        if seq_mode:
            u_scr[8 + r * rows : 8 + (r + 1) * rows, :] = u
            prev2 = u_scr[6 + r * rows : 6 + (r + 1) * rows, :]
            prev1 = u_scr[7 + r * rows : 7 + (r + 1) * rows, :]
        else:
            prev2 = p0_ref[0, sl, :]
            prev1 = p1_ref[0, sl, :]
            u_ref[0, sl, :] = u
        y = ck[0:1, :] * prev2 + ck[1:2, :] * prev1 + ck[2:3, :] * u
        o = _dot((bg * y).astype(BF), wout_ref[...])
        y_ref[0, sl, :] = _post(x, o, g_ref, mr, 1, 1.0)
    if seq_mode:
        tail = u_scr[6 + tm : 8 + tm, :]
        u_scr[6:8, :] = tail
        u_ref[0] = tail


def _conv_block(x, mod, norm_g, w_in, conv_k, w_out, prev0, prev1, layer, tm, seq_mode):
    bx, L, d = x.shape
    x_spec, mod_spec, g_spec = _row_specs(bx, tm, d, mod, layer, 1)
    if seq_mode:
        p_spec = pl.BlockSpec((1, 1, d), lambda b, i: (b, 0, 0))
        u_spec = pl.BlockSpec((1, 2, d), lambda b, i: (b, 0, 0))
        u_shape = jax.ShapeDtypeStruct((bx, 2, d), F32)
    else:
        p_spec = pl.BlockSpec((1, tm, d), lambda b, i: (b, i, 0))
        u_spec = pl.BlockSpec((1, tm, d), lambda b, i: (b, i, 0))
        u_shape = jax.ShapeDtypeStruct((bx, L, d), F32)
    return pl.pallas_call(
        functools.partial(_conv_kernel, seq_mode=seq_mode, tm=tm, parts=2 if tm % 512 == 0 else 1),
        grid=(bx, L // tm),
        in_specs=[
            x_spec,
            mod_spec,
            g_spec,
            pl.BlockSpec((None, d, 3 * d), lambda b, i: (layer, 0, 0)),
            pl.BlockSpec((None, CONV_W, d), lambda b, i: (layer, 0, 0)),
            pl.BlockSpec((None, d, d), lambda b, i: (layer, 0, 0)),
            p_spec,
            p_spec,
        ],
        out_specs=[pl.BlockSpec((1, tm, d), lambda b, i: (b, i, 0)), u_spec],
        out_shape=[jax.ShapeDtypeStruct((bx, L, d), F32), u_shape],
        scratch_shapes=[pltpu.VMEM((tm + 8, d), F32)],
        compiler_params=_cp(("parallel", "arbitrary")),
        name="conv_block",
    )(x, mod, norm_g, w_in, conv_k, w_out, prev0, prev1)


def _kv_kernel(x_ref, g_ref, w_ref, cos_ref, sin_ref, rows_ref, win_ref):
    hn = _rms(x_ref[0], g_ref[...]).astype(BF)
    kv = _dot(hn, w_ref[...])
    cos = cos_ref[...]
    sin = sin_ref[...]
    rows_ref[0, :, 0 : 2 * KV_W] = kv[:, 0 : 2 * KV_W]
    rows_ref[0, :, 2 * KV_W : 3 * KV_W] = _rope(kv[:, 2 * KV_W : 3 * KV_W], cos, sin)
    rows_ref[0, :, 3 * KV_W : 4 * KV_W] = kv[:, 3 * KV_W : 4 * KV_W]
    win_ref[0, :, 0:KV_W] = _rope(kv[:, 4 * KV_W : 5 * KV_W], cos, sin)
    win_ref[0, :, KV_W : 2 * KV_W] = kv[:, 5 * KV_W : 6 * KV_W]


def _kv_proj(x, kv_norm_g, w_kv, cos, sin, tm):
    bx, L, d = x.shape
    return pl.pallas_call(
        _kv_kernel,
        grid=(bx, L // tm),
        in_specs=[
            pl.BlockSpec((1, tm, d), lambda b, i: (b, i, 0)),
            pl.BlockSpec((1, d), lambda b, i: (0, 0)),
            pl.BlockSpec((d, 6 * KV_W), lambda b, i: (0, 0)),
            pl.BlockSpec((tm, KV_W), lambda b, i: (i, 0)),
            pl.BlockSpec((tm, KV_W), lambda b, i: (i, 0)),
        ],
        out_specs=[
            pl.BlockSpec((1, tm, 4 * KV_W), lambda b, i: (b, i, 0)),
            pl.BlockSpec((1, tm, 2 * KV_W), lambda b, i: (b, i, 0)),
        ],
        out_shape=[
            jax.ShapeDtypeStruct((bx, L, 4 * KV_W), F32),
            jax.ShapeDtypeStruct((bx, L, 2 * KV_W), F32),
        ],
        compiler_params=_cp(("parallel", "parallel")),
        name="kv_proj",
    )(x, kv_norm_g.reshape(1, d), w_kv, cos, sin)


GATE_ROWS = 16


def _qg_kernel(x_ref, mod_ref, g_ref, wq_ref, wg_ref, cos_ref, sin_ref, qn_ref, qr_ref, gate_ref, *, lanes_major):
    h = _pre(x_ref[0], g_ref, mod_ref[...], 1).astype(BF)
    q = _dot(h, wq_ref[...])
    gates = jax.nn.sigmoid(_dot(h, wg_ref[...]))
    scale = HEAD_DIM**-0.5 * LOG2E
    cos = cos_ref[...]
    sin = sin_ref[...]
    if lanes_major:
        gate_ref[0] = gates.T[0 : gate_ref.shape[1], :]
        qn_ref[0] = (q * scale).T.astype(BF)
    else:
        gate_ref[0] = gates
        qn_ref[0] = (q * scale).astype(BF)
    for c in range(q.shape[1] // KV_W):
        sl = slice(c * KV_W, (c + 1) * KV_W)
        qr = _rope(q[:, sl], cos, sin) * scale
        if lanes_major:
            qr_ref[0, sl, :] = qr.T.astype(BF)
        else:
            qr_ref[0, :, sl] = qr.astype(BF)


def _qg_proj(x, mod, norm_g, wq, wg, cos, sin, layer, tm, lanes_major):
    bx, L, d = x.shape
    nq = wq.shape[-1]
    x_spec, mod_spec, g_spec = _row_specs(bx, tm, d, mod, layer, 1)
    if lanes_major:
        ng = N_KV_HEADS * GATE_ROWS
        out_specs = [
            pl.BlockSpec((1, nq, tm), lambda b, i: (b, 0, i)),
            pl.BlockSpec((1, nq, tm), lambda b, i: (b, 0, i)),
            pl.BlockSpec((1, ng, tm), lambda b, i: (b, 0, i)),
        ]
        out_shape = [
            jax.ShapeDtypeStruct((bx, nq, L), BF),
            jax.ShapeDtypeStruct((bx, nq, L), BF),
            jax.ShapeDtypeStruct((bx, ng, L), F32),
        ]
    else:
        out_specs = [
            pl.BlockSpec((1, tm, nq), lambda b, i: (b, i, 0)),
            pl.BlockSpec((1, tm, nq), lambda b, i: (b, i, 0)),
            pl.BlockSpec((1, tm, 128), lambda b, i: (b, i, 0)),
        ]
        out_shape = [
            jax.ShapeDtypeStruct((bx, L, nq), BF),
            jax.ShapeDtypeStruct((bx, L, nq), BF),
            jax.ShapeDtypeStruct((bx, L, 128), F32),
        ]
    return pl.pallas_call(
        functools.partial(_qg_kernel, lanes_major=lanes_major),
        grid=(bx, L // tm),
        in_specs=[
            x_spec,
            mod_spec,
            g_spec,
            pl.BlockSpec((d, nq), lambda b, i: (0, 0)),
            pl.BlockSpec((d, 128), lambda b, i: (0, 0)),
            pl.BlockSpec((tm, KV_W), lambda b, i: (i, 0)),
            pl.BlockSpec((tm, KV_W), lambda b, i: (i, 0)),
        ],
        out_specs=out_specs,
        out_shape=out_shape,
        compiler_params=_cp(("parallel", "parallel")),
        name="qg_proj",
    )(x, mod, norm_g, wq, wg, cos, sin)


def _op_kernel(o_ref, x_ref, mod_ref, g_ref, w_ref, y_ref, *, lanes_major):
    if lanes_major:
        a = lax.dot_general(o_ref[0], w_ref[...], (((0,), (0,)), ((), ())), preferred_element_type=F32)
    else:
        a = _dot(o_ref[0], w_ref[...])
    y_ref[0] = _post(x_ref[0], a, g_ref, mod_ref[...], 1, 1.0)


def _out_proj(o, x, mod, norm_g, w_o, layer, tm, lanes_major):
    bx, L, d = x.shape
    x_spec, mod_spec, g_spec = _row_specs(bx, tm, d, mod, layer, 1)
    if lanes_major:
        o_spec = pl.BlockSpec((1, o.shape[1], tm), lambda b, i: (b, 0, i))
    else:
        o_spec = pl.BlockSpec((1, tm, o.shape[-1]), lambda b, i: (b, i, 0))
    return pl.pallas_call(
        functools.partial(_op_kernel, lanes_major=lanes_major),
        grid=(bx, L // tm),
        in_specs=[
            o_spec,
            x_spec,
            mod_spec,
            g_spec,
            pl.BlockSpec(w_o.shape, lambda b, i: (0, 0)),
        ],
        out_specs=pl.BlockSpec((1, tm, d), lambda b, i: (b, i, 0)),
        out_shape=jax.ShapeDtypeStruct((bx, L, d), F32),
        compiler_params=_cp(("parallel", "parallel")),
        name="out_proj",
    )(o, x, mod, norm_g, w_o)


def _compress_tile(get_x, pe_ref, w1_ref, w2_ref, b_scr, m_rows):
    half = CMP_LEN // 2
    xs = [get_x(l) for l in range(half)]
    xa = jnp.concatenate([(xs[l] + pe_ref[l]).astype(BF) for l in range(half)], axis=1)
    xb = jnp.concatenate([(xs[l] + pe_ref[half + l]).astype(BF) for l in range(half)], axis=1)
    a = _dot(xa, w1_ref[0])
    b_scr[0:m_rows, :] = _dot(xb, w1_ref[1])
    b_scr[m_rows : m_rows + 8, :] = jnp.zeros((8, b_scr.shape[1]), F32)
    hid = a + b_scr[1 : m_rows + 1, :]
    return _dot(_silu(hid).astype(BF), w2_ref[...])


def _cmp_prompt_kernel(x_ref, pe_ref, w1_ref, w2_ref, o_ref, b_scr, *, ng, nc):
    row = lax.broadcasted_iota(jnp.int32, (ng, 128), 0)
    res = _compress_tile(
        lambda l: x_ref[0, pl.ds(l, ng, stride=CMP_STRIDE), :], pe_ref, w1_ref, w2_ref, b_scr, ng
    )
    o_ref[0] = jnp.where(row < nc, res, 0.0)


def _cmp_weights(cmp_pe, cmp_w1, cmp_w2):
    z1 = jnp.zeros_like(cmp_w1)
    w1bd = jnp.concatenate(
        [jnp.concatenate([cmp_w1, z1], axis=3), jnp.concatenate([z1, cmp_w1], axis=3)], axis=2
    )
    hid2 = w1bd.shape[-1]
    w1bd = w1bd.reshape(2, 2, (CMP_LEN // 2) * 2 * HEAD_DIM, hid2).astype(BF)
    z2 = jnp.zeros_like(cmp_w2)
    w2bd = jnp.concatenate(
        [jnp.concatenate([cmp_w2, z2], axis=2), jnp.concatenate([z2, cmp_w2], axis=2)], axis=1
    ).astype(BF)
    pe2 = jnp.concatenate([cmp_pe, cmp_pe], axis=2)[:, :, None, :]
    return pe2, w1bd, w2bd


def _compress_prompt(rows, pe2, w1bd, w2bd):
    bx, L, _ = rows.shape
    ng = L // CMP_STRIDE
    nc = (L - CMP_LEN) // CMP_STRIDE + 1
    kdim, hid2 = w1bd.shape[2], w1bd.shape[3]
    return pl.pallas_call(
        functools.partial(_cmp_prompt_kernel, ng=ng, nc=nc),
        grid=(4, bx),
        in_specs=[
            pl.BlockSpec((1, L, 128), lambda cg, b: (b, 0, cg)),
            pl.BlockSpec((None, CMP_LEN, 1, 128), lambda cg, b: (cg // 2, 0, 0, 0)),
            pl.BlockSpec((None, 2, kdim, hid2), lambda cg, b: (cg // 2, 0, 0, 0)),
            pl.BlockSpec((None, hid2, 128), lambda cg, b: (cg // 2, 0, 0)),
        ],
        out_specs=pl.BlockSpec((1, ng, 128), lambda cg, b: (b, 0, cg)),
        out_shape=jax.ShapeDtypeStruct((bx, ng, 2 * KV_W), F32),
        scratch_shapes=[pltpu.VMEM((ng + 8, hid2), F32)],
        compiler_params=_cp(("parallel", "parallel")),
        name="compress_prompt",
    )(rows, pe2, w1bd, w2bd)


def _cmp_sample_kernel(pt_ref, *refs, n_pages, sb, gpp, nc):
    del pt_ref
    page_refs = refs[: sb * n_pages]
    pe_ref, w1_ref, w2_ref, o_ref, b_scr, x_scr = refs[sb * n_pages :]
    ng = n_pages * gpp
    m_rows = sb * ng
    page = page_refs[0].shape[1]
    row = lax.broadcasted_iota(jnp.int32, (ng, 128), 0)
    for c in range(KV_W // 128):
        lanes = slice(128 * c, 128 * (c + 1))
        for i, r in enumerate(page_refs):
            x_scr[i * page : (i + 1) * page, :] = r[0, :, lanes].astype(F32)
        res = _compress_tile(
            lambda l: x_scr[pl.ds(l, m_rows, stride=CMP_STRIDE), :], pe_ref, w1_ref, w2_ref, b_scr, m_rows
        )
        for s in range(sb):
            o_ref[s, :, lanes] = jnp.where(row < nc, res[s * ng : (s + 1) * ng, :], 0.0)


def _compress_sample(cache2, page_table, pe2, w1bd, w2bd, total_len, sb):
    db, n_pages = page_table.shape
    page = cache2.shape[1]
    gpp = page // CMP_STRIDE
    ng = n_pages * gpp
    nc = (total_len - CMP_LEN) // CMP_STRIDE + 1
    assert nc <= ng
    kdim, hid2 = w1bd.shape[2], w1bd.shape[3]

    def page_spec(s, p):
        return pl.BlockSpec((1, page, KV_W), lambda kv, bb, pt: (pt[bb * sb + s, p], 0, kv))

    grid_spec = pltpu.PrefetchScalarGridSpec(
        num_scalar_prefetch=1,
        grid=(2, db // sb),
        in_specs=[page_spec(s, p) for s in range(sb) for p in range(n_pages)]
        + [
            pl.BlockSpec((None, CMP_LEN, 1, 128), lambda kv, bb, pt: (kv, 0, 0, 0)),
            pl.BlockSpec((None, 2, kdim, hid2), lambda kv, bb, pt: (kv, 0, 0, 0)),
            pl.BlockSpec((None, hid2, 128), lambda kv, bb, pt: (kv, 0, 0)),
        ],
        out_specs=pl.BlockSpec((sb, ng, KV_W), lambda kv, bb, pt: (bb, 0, kv)),
        scratch_shapes=[pltpu.VMEM((sb * ng + 8, hid2), F32), pltpu.VMEM((sb * n_pages * page, 128), F32)],
    )
    return pl.pallas_call(
        functools.partial(_cmp_sample_kernel, n_pages=n_pages, sb=sb, gpp=gpp, nc=nc),
        grid_spec=grid_spec,
        out_shape=jax.ShapeDtypeStruct((db, ng, 2 * KV_W), F32),
        compiler_params=_cp(("parallel", "parallel")),
        name="compress_sample",
    )(page_table, *([cache2] * (sb * n_pages)), pe2, w1bd, w2bd)


def _select_scores(imp, blk, pos):
    cur = pos // SLC_BLOCK
    valid = blk * SLC_BLOCK <= pos
    forced = (blk == 0) | (blk == cur) | (blk == cur - 1)
    return jnp.where(valid, imp + FORCE_BONUS * forced.astype(F32), -1.0)


def _topk_rows(score, k):
    n = score.shape[0]
    ridx = lax.broadcasted_iota(jnp.int32, score.shape, 0).astype(F32)

    def body(_, sc):
        m = jnp.max(sc, axis=0, keepdims=True)
        first = jnp.min(jnp.where(sc == m, ridx, float(n)), axis=0, keepdims=True)
        return jnp.where(ridx == first, -jnp.inf, sc)

    sc = lax.fori_loop(0, k, body, score, unroll=True)
    return (sc == -jnp.inf) & (score >= 0.0)


BIAS_ROWS = 16


def _masked_attend(k_rows, q, bias, v_aug):
    s = _dot(k_rows, q) + jnp.concatenate([bias] * GROUP, axis=1)
    e = jnp.exp2(s - jnp.max(s, axis=0, keepdims=True))
    return e, _dot(v_aug, e.astype(BF))


def _attn_prompt_kernel(
    qn_ref, qr_ref, g_ref, kc_ref, vct_ref, ks_ref, vst_ref, kw_ref, vwt_ref, ovt_ref, ctab_ref, wtab_ref, tri_ref,
    o_ref, bias_scr, qaug_scr, sa_scr, sb_scr, m_scr, acc_scr, *, tq, tk, ns, top_n,
):
    start = pl.program_id(2) * tq
    gq = GROUP * tq
    bpc = tk // SLC_BLOCK

    def heads_on_lanes(ref):
        t = ref[0]
        return jnp.concatenate([t[HEAD_DIM * g : HEAD_DIM * (g + 1), :] for g in range(GROUP)], axis=1)

    qn = heads_on_lanes(qn_ref)
    qr = heads_on_lanes(qr_ref)
    pos = start + lax.broadcasted_iota(jnp.int32, (1, tq), 1)
    pos_g = jnp.concatenate([pos] * GROUP, axis=1)

    ngp = kc_ref.shape[2]
    c0 = pl.multiple_of(ngp - start // CMP_STRIDE, 8)

    e_c, un_c = _masked_attend(kc_ref[0, 0], qn, ctab_ref[pl.ds(c0, ngp), :], vct_ref[0, 0])
    inv_c = jnp.where(pos_g >= CMP_LEN - 1, 1.0 / jnp.maximum(un_c[HEAD_DIM : HEAD_DIM + 1, :], 1e-30), 0.0)
    o_c = un_c[0:HEAD_DIM, :] * inv_c
    psum = e_c[:, 0:tq] * inv_c[:, 0:tq]
    for g in range(1, GROUP):
        psum = psum + e_c[:, g * tq : (g + 1) * tq] * inv_c[:, g * tq : (g + 1) * tq]
    imp = _split_dot(ovt_ref[...], psum)

    blk = lax.broadcasted_iota(jnp.int32, (NS_PAD, 1), 0)
    score = jnp.where(blk < ns, _select_scores(imp, blk, pos), -jnp.inf)
    sel_bias = jnp.where(_topk_rows(score, top_n), 0.0, NEG)
    for c in range(NS_PAD // bpc):
        bias_scr[BIAS_ROWS * c : BIAS_ROWS * c + bpc, :] = sel_bias[bpc * c : bpc * (c + 1), :]
        if bpc < BIAS_ROWS:
            bias_scr[BIAS_ROWS * c + bpc : BIAS_ROWS * (c + 1), :] = jnp.zeros((BIAS_ROWS - bpc, tq), F32)

    wk = WINDOW + tq
    w0 = pl.multiple_of(jnp.maximum(start - WINDOW, 0), 128)
    t0 = pl.multiple_of(WINDOW - (start - w0), 128)
    _, un_w = _masked_attend(
        kw_ref[0, 0, pl.ds(w0, wk), :], qr, wtab_ref[pl.ds(t0, wk), :], vwt_ref[0, 0, :, pl.ds(w0, wk)]
    )
    o_w = un_w[0:HEAD_DIM, :] * (1.0 / jnp.maximum(un_w[HEAD_DIM : HEAD_DIM + 1, :], 1e-30))

    qaug_scr[0:HEAD_DIM, :] = qr
    qaug_scr[HEAD_DIM:, :] = jnp.zeros((qaug_scr.shape[0] - HEAD_DIM, gq), BF)
    m_scr[...] = jnp.full(m_scr.shape, NEG, F32)
    acc_scr[...] = jnp.zeros(acc_scr.shape, F32)

    def scores(c, s_ref):
        k0 = pl.multiple_of(c * tk, tk)
        b = bias_scr[pl.ds(pl.multiple_of(c * BIAS_ROWS, BIAS_ROWS), BIAS_ROWS), :]
        qaug_scr[HEAD_DIM : HEAD_DIM + BIAS_ROWS, :] = jnp.concatenate([b] * GROUP, axis=1).astype(BF)
        s_ref[...] = _dot(ks_ref[0, 0, pl.ds(k0, tk), :], qaug_scr[...])

    def accumulate(c, s_ref, causal):
        k0 = pl.multiple_of(c * tk, tk)
        if causal:
            off = pl.multiple_of(start - k0, tq)
            s_ref[pl.ds(off, tq), :] = s_ref[pl.ds(off, tq), :] + jnp.concatenate([tri_ref[...]] * GROUP, axis=1)
        s = s_ref[...]
        m_old = m_scr[...]
        m_new = jnp.maximum(m_old, jnp.max(s, axis=0, keepdims=True))
        p = jnp.exp2(s - m_new).astype(BF)
        acc_scr[...] = jnp.exp2(m_old - m_new) * acc_scr[...] + _dot(vst_ref[0, 0, :, pl.ds(k0, tk)], p)
        m_scr[...] = m_new

    def pair(j, carry):
        scores(2 * j + 1, sb_scr)
        accumulate(2 * j, sa_scr, False)
        scores(2 * j + 2, sa_scr)
        accumulate(2 * j + 1, sb_scr, False)
        return carry

    last = (start + tq - 1) // tk
    scores(0, sa_scr)
    lax.fori_loop(0, last // 2, pair, 0)

    @pl.when(last % 2 == 0)
    def _():
        accumulate(last, sa_scr, True)

    @pl.when(last % 2 == 1)
    def _():
        scores(last, sb_scr)
        accumulate(last - 1, sa_scr, False)
        accumulate(last, sb_scr, True)

    acc = acc_scr[...]
    o_s = acc[0:HEAD_DIM, :] * (1.0 / jnp.maximum(acc[HEAD_DIM : HEAD_DIM + 1, :], 1e-30))

    gates = g_ref[0]
    outs = []
    for g in range(GROUP):
        sl = slice(g * tq, (g + 1) * tq)
        outs.append(
            gates[3 * g : 3 * g + 1, :] * o_c[:, sl]
            + gates[3 * g + 1 : 3 * g + 2, :] * o_s[:, sl]
            + gates[3 * g + 2 : 3 * g + 3, :] * o_w[:, sl]
        )
    o_ref[0] = jnp.concatenate(outs, axis=0).astype(BF)


def _attn_prompt(qn_t, qr_t, gates_t, kc, vc_t, ks, vs_t, kw, vw_t, ov_t, ctab, wtab, tri, ns, tq, tk):
    bx, _, L = qn_t.shape
    ngp = kc.shape[2]
    gd = GROUP * HEAD_DIM
    kaug = ks.shape[-1]
    vaug = vs_t.shape[2]
    kv_spec = pl.BlockSpec((1, 1, L, HEAD_DIM), lambda b, k, i: (b, k, 0, 0))
    vt_spec = pl.BlockSpec((1, 1, vaug, L), lambda b, k, i: (b, k, 0, 0))
    q_spec = pl.BlockSpec((1, gd, tq), lambda b, k, i: (b, k, i))
    return pl.pallas_call(
        functools.partial(_attn_prompt_kernel, tq=tq, tk=tk, ns=ns, top_n=min(TOP_N, ns)),
        grid=(bx, N_KV_HEADS, L // tq),
        in_specs=[
            q_spec,
            q_spec,
            pl.BlockSpec((1, GATE_ROWS, tq), lambda b, k, i: (b, k, i)),
            pl.BlockSpec((1, 1, ngp, HEAD_DIM), lambda b, k, i: (b, k, 0, 0)),
            pl.BlockSpec((1, 1, vaug, ngp), lambda b, k, i: (b, k, 0, 0)),
            pl.BlockSpec((1, 1, L, kaug), lambda b, k, i: (b, k, 0, 0)),
            vt_spec,
            kv_spec,
            vt_spec,
            pl.BlockSpec(ov_t.shape, lambda b, k, i: (0, 0)),
            pl.BlockSpec(ctab.shape, lambda b, k, i: (0, 0)),
            pl.BlockSpec(wtab.shape, lambda b, k, i: (0, 0)),
            pl.BlockSpec(tri.shape, lambda b, k, i: (0, 0)),
        ],
        out_specs=q_spec,
        out_shape=jax.ShapeDtypeStruct(qn_t.shape, BF),
        scratch_shapes=[
            pltpu.VMEM((NS_PAD // (tk // SLC_BLOCK) * BIAS_ROWS, tq), F32),
            pltpu.VMEM((kaug, GROUP * tq), BF),
            pltpu.VMEM((tk, GROUP * tq), F32),
            pltpu.VMEM((tk, GROUP * tq), F32),
            pltpu.VMEM((1, GROUP * tq), F32),
            pltpu.VMEM((vaug, GROUP * tq), F32),
        ],
        compiler_params=_cp(("parallel", "parallel", "arbitrary")),
        name="attn_prompt",
    )(qn_t, qr_t, gates_t, kc, vc_t, ks, vs_t, kw, vw_t, ov_t, ctab, wtab, tri)


def _softmax_rows_ext(s, mask, s_new, m_new_mask):
    sm = jnp.where(mask, s, NEG)
    sn = jnp.where(m_new_mask, s_new, NEG)
    mx = jnp.maximum(jnp.max(sm, axis=1, keepdims=True), sn)
    e = jnp.exp2(sm - mx) * mask.astype(F32)
    en = jnp.exp2(sn - mx) * m_new_mask.astype(F32)
    inv = 1.0 / jnp.maximum(jnp.sum(e, axis=1, keepdims=True) + en, 1e-30)
    return e * inv, en * inv


def _attn_sample_kernel(pt_ref, *refs, n_pages, sb, **static):
    del pt_ref
    rest = refs[sb * n_pages :]
    stages = [_attn_sample_stages(s, refs[s * n_pages : (s + 1) * n_pages], *rest, **static) for s in range(sb)]
    for _ in range(4):
        for st in stages:
            next(st)


def _attn_sample_stages(
    s_idx, page_refs, qn_ref, qr_ref, g_ref, cmp_ref, new_ref, win_ref, wnew_ref, ov_ref, exp_ref, gsum_ref, o_ref,
    *, page, past_len, nc, ns, top_n, win_buf,
):
    pos = past_len
    qn = qn_ref[s_idx]
    qr = qr_ref[s_idx]
    qr32 = qr.astype(F32)
    nh = qn.shape[0]

    cmp = cmp_ref[s_idx]
    kc = cmp[:, 0:KV_W].astype(BF)
    vc = cmp[:, KV_W : 2 * KV_W].astype(BF)
    ngp = kc.shape[0]
    new = new_ref[s_idx]
    k_new = new[:, 2 * KV_W : 3 * KV_W]
    v_new = new[:, 3 * KV_W : 4 * KV_W]
    wbuf = win_ref[s_idx]
    wnew = wnew_ref[s_idx]
    s_c = _dot_nt(qn, kc)
    s_w = _dot_nt(qr, wbuf[:, 0:KV_W].astype(BF))
    s_wn = jnp.sum(qr32 * wnew[:, 0:KV_W].astype(BF).astype(F32), axis=1, keepdims=True)
    s = jnp.concatenate([_dot_nt(qr, r[0][:, 0:KV_W].astype(BF)) for r in page_refs], axis=1)
    s_new = jnp.sum(qr32 * k_new.astype(BF).astype(F32), axis=1, keepdims=True)
    yield

    n_idx = lax.broadcasted_iota(jnp.int32, (1, ngp), 1)
    cmask = jnp.broadcast_to((n_idx * CMP_STRIDE + (CMP_LEN - 1) <= pos) & (n_idx < nc), (nh, ngp))
    sm = jnp.where(cmask, s_c, NEG)
    e = jnp.exp2(sm - jnp.max(sm, axis=1, keepdims=True)) * cmask.astype(F32)
    p_c = e / jnp.maximum(jnp.sum(e, axis=1, keepdims=True), 1e-30)
    o_c = _dot(p_c.astype(BF), vc)
    t = _split_dot_r(p_c, ov_ref[...])
    imp = _split_dot(gsum_ref[...], t)
    yield

    blk = lax.broadcasted_iota(jnp.int32, (1, NS_PAD), 1)
    score = jnp.where(blk < ns, _select_scores(imp, blk, pos), -jnp.inf)
    pad = jnp.concatenate([score, jnp.zeros((NS_PAD - nh, NS_PAD), F32)], axis=0)
    score_t = pad.T
    ii = lax.broadcasted_iota(jnp.int32, (NS_PAD, NS_PAD), 0)
    jj = lax.broadcasted_iota(jnp.int32, (NS_PAD, NS_PAD), 1)
    head_grp = lax.broadcasted_iota(jnp.int32, (nh, NS_PAD), 0) // GROUP
    sel = jnp.zeros((nh, NS_PAD), F32)
    for k in range(nh // GROUP):
        h = k * GROUP
        a = score_t[:, h : h + 1]
        b = score[h : h + 1, :]
        beats = (a > b) | ((a == b) & (ii < jj))
        rank = jnp.sum(beats.astype(F32), axis=0, keepdims=True)
        sel_k = jnp.where((rank < float(top_n)) & (b >= 0.0), 1.0, 0.0)
        sel = jnp.where(head_grp == k, sel_k, sel)

    kmask = _dot(sel.astype(BF), exp_ref[...]) > 0.5
    nb = pos // SLC_BLOCK
    new_sel = sel[:, nb : nb + 1] > 0.5
    yield

    kw_pos = (past_len - win_buf) + lax.broadcasted_iota(jnp.int32, (1, win_buf), 1)
    rel = pos - kw_pos
    wmask = jnp.broadcast_to((rel >= 0) & (rel < WINDOW) & (kw_pos >= 0), (nh, win_buf))
    p_w, p_wn = _softmax_rows_ext(s_w, wmask, s_wn, jnp.full((nh, 1), True))
    o_w = _dot(p_w.astype(BF), wbuf[:, KV_W : 2 * KV_W].astype(BF)) + p_wn * wnew[:, KV_W : 2 * KV_W].astype(BF).astype(F32)
    p_s, p_new = _softmax_rows_ext(s, kmask, s_new, new_sel)
    o_s = p_new * v_new.astype(BF).astype(F32)
    for i, r in enumerate(page_refs):
        o_s = o_s + _dot(p_s[:, i * page : (i + 1) * page].astype(BF), r[0][:, KV_W : 2 * KV_W].astype(BF))
    gates = g_ref[s_idx]
    o_ref[s_idx] = gates[:, 0:1] * o_c + gates[:, 1:2] * o_s + gates[:, 2:3] * o_w
    yield


def _split_dot_r(a_f32, b):
    hi = a_f32.astype(BF)
    lo = (a_f32 - hi.astype(F32)).astype(BF)
    return _dot(hi, b) + _dot(lo, b)


def _attn_sample(cache2, page_table, qn_bd, qr_bd, gates, cmp_s, rows_new, win_state, win_new, ov, expand, gsum, nc, ns):
    db, n_pages = page_table.shape
    page = cache2.shape[1]
    past_len = n_pages * page
    win_buf = win_state.shape[1]
    ngp = cmp_s.shape[1]
    sb = _pick_tile(db, 2)

    def one(shape):
        return pl.BlockSpec((sb,) + shape, lambda b, pt: (b, 0, 0))

    def const(a):
        return pl.BlockSpec(a.shape, lambda b, pt: (0, 0))

    def page_spec(s, p):
        return pl.BlockSpec((1, page, 2 * KV_W), lambda b, pt: (pt[b * sb + s, p], 0, 1))

    grid_spec = pltpu.PrefetchScalarGridSpec(
        num_scalar_prefetch=1,
        grid=(db // sb,),
        in_specs=[page_spec(s, p) for s in range(sb) for p in range(n_pages)]
        + [
            one((N_HEADS, KV_W)),
            one((N_HEADS, KV_W)),
            one((N_HEADS, 3)),
            one((ngp, 2 * KV_W)),
            one((1, 4 * KV_W)),
            one((win_buf, 2 * KV_W)),
            one((1, 2 * KV_W)),
            const(ov),
            const(expand),
            const(gsum),
        ],
        out_specs=one((N_HEADS, KV_W)),
    )
    return pl.pallas_call(
        functools.partial(
            _attn_sample_kernel, n_pages=n_pages, sb=sb, page=page, past_len=past_len, nc=nc, ns=ns,
            top_n=min(TOP_N, ns), win_buf=win_buf,
        ),
        grid_spec=grid_spec,
        out_shape=jax.ShapeDtypeStruct((db, N_HEADS, KV_W), F32),
        compiler_params=_cp(("parallel",)),
        name="attn_sample",
    )(page_table, *([cache2] * (sb * n_pages)), qn_bd, qr_bd, gates, cmp_s, rows_new, win_state, win_new, ov, expand, gsum)


def _rope_tables(pos):
    half = HEAD_DIM // 2
    inv = ROPE_THETA ** (-jnp.arange(half, dtype=F32) / half)
    ang = pos.astype(F32)[:, None] * inv[None]
    cos, sin = jnp.cos(ang), jnp.sin(ang)
    cos = jnp.tile(jnp.concatenate([cos, cos], axis=1), (1, N_KV_HEADS))
    sin = jnp.tile(jnp.concatenate([-sin, sin], axis=1), (1, N_KV_HEADS))
    return cos, sin


def _overlap(n_rows, n_cols):
    cs = jnp.arange(n_rows)[:, None] * CMP_STRIDE
    ss = jnp.arange(n_cols)[None] * SLC_BLOCK
    return ((cs < ss + SLC_BLOCK) & (cs + CMP_LEN > ss)).astype(F32)


def _pick_tile(n, pref):
    t = min(n, pref)
    while n % t:
        t //= 2
    return t


def kernel(x_prompt, x_sample, cache_kv, state_win_kv, state_conv, page_table, c_prompt, c_sample,
           ada_w, ada_b, norm_g, ffn_w_in, ffn_w_out, conv_w_in, conv_k, conv_w_out,
           kv_norm_g, w_kv, cmp_pe, cmp_w1, cmp_w2, nsa_w_qg, nsa_w_o):
    bp, seq, d = x_prompt.shape
    db, dec_seq, _ = x_sample.shape
    assert dec_seq == 1
    depth = ada_w.shape[0]
    n_a = conv_w_in.shape[0]
    n_pages = page_table.shape[1]
    page = cache_kv.shape[1]
    past_len = n_pages * page
    win_buf = state_win_kv.shape[1]
    dff = ffn_w_out.shape[2]
    nq = N_HEADS * HEAD_DIM

    ffn_w_in_b = ffn_w_in.astype(BF)
    ffn_w_out_b = ffn_w_out.astype(BF)
    conv_w_in_b = conv_w_in.astype(BF)
    conv_w_out_b = conv_w_out.astype(BF)
    w_kv_b = w_kv.astype(BF)
    wq_b = nsa_w_qg[:, :, :nq].astype(BF)
    wg_b = jnp.pad(
        nsa_w_qg[:, :, nq:].reshape(nsa_w_qg.shape[0], d, N_KV_HEADS, 3 * GROUP),
        ((0, 0), (0, 0), (0, 0), (0, GATE_ROWS - 3 * GROUP)),
    ).reshape(nsa_w_qg.shape[0], d, N_KV_HEADS * GATE_ROWS)
    wg_b = jnp.pad(wg_b, ((0, 0), (0, 0), (0, 128 - N_KV_HEADS * GATE_ROWS))).astype(BF)
    w_o_b = nsa_w_o.astype(BF)
    pe2, w1bd, w2bd = _cmp_weights(cmp_pe, cmp_w1, cmp_w2)

    r_pad = -(db + bp) % 8
    c_all = jnp.concatenate([c_sample, c_prompt, jnp.zeros((r_pad, d), F32)], axis=0)
    mod_all = _ada_mod(c_all, ada_w, ada_b)
    mod_s = mod_all[:, :, None, :db, :]
    mod_p = mod_all[:, :, db : db + bp, None, :]

    def trunk(x, mod, tm, conv_prev, seq_mode, build_kv, attend):
        conv_states = []
        parts = 2 if tm % 512 == 0 else 1
        for l in range(depth):
            x = _ffn_block(x, mod, norm_g, ffn_w_in_b, ffn_w_out_b, l, 0, 0, tm, parts)
            if l < n_a:
                p0, p1 = conv_prev(l)
                x, st = _conv_block(x, mod, norm_g, conv_w_in_b, conv_k, conv_w_out_b, p0, p1, l, tm, seq_mode)
                conv_states.append(st)
            else:
                j = l - n_a
                x = attend(x, mod, l, j)
            x = _ffn_block(x, mod, norm_g, ffn_w_in_b, ffn_w_out_b, l, 2, 1, tm, parts)
            if l == n_a - 1:
                build_kv(x)
        return x, conv_states

    tm_p = _pick_tile(seq, 512)
    cos_p, sin_p = _rope_tables(jnp.arange(seq, dtype=jnp.int32))
    ng_p = seq // CMP_STRIDE
    ns_p = -(-seq // SLC_BLOCK)
    assert ns_p <= NS_PAD and seq % Q_BLOCK == 0 and seq >= WINDOW + Q_BLOCK
    ov_t_p = _overlap(ng_p, NS_PAD).T.astype(BF)
    tk_p = _pick_tile(seq, BIAS_ROWS * SLC_BLOCK)
    qoff = jnp.arange(Q_BLOCK)[None, :]
    cu = jnp.arange(2 * ng_p)[:, None] - ng_p
    ctab_p = jnp.where(cu * CMP_STRIDE + (CMP_LEN - 1) <= qoff, 0.0, NEG).astype(F32)
    wu = jnp.arange(2 * WINDOW + Q_BLOCK)[:, None]
    wtab_p = jnp.where((wu > qoff) & (wu <= qoff + WINDOW), 0.0, NEG).astype(F32)
    tri_p = jnp.where(jnp.arange(Q_BLOCK)[:, None] <= qoff, 0.0, NEG).astype(F32)
    ctx_p = {}

    def build_prompt(x):
        rows, win = _kv_proj(x, kv_norm_g, w_kv_b, cos_p, sin_p, tm_p)
        cmp = _compress_prompt(rows, pe2, w1bd, w2bd)

        def heads_major(a):
            return a.reshape(bp, a.shape[1], N_KV_HEADS, HEAD_DIM).transpose(0, 2, 1, 3).astype(BF)

        def values_t(a):
            n = a.shape[1]
            ones_rows = jnp.broadcast_to((jnp.arange(BIAS_ROWS)[:, None] == 0).astype(BF), (BIAS_ROWS, n))
            return jnp.concatenate(
                [
                    a.transpose(0, 2, 1).astype(BF).reshape(bp, N_KV_HEADS, HEAD_DIM, n),
                    jnp.broadcast_to(ones_rows, (bp, N_KV_HEADS, BIAS_ROWS, n)),
                ],
                axis=2,
            )

        blk_in_chunk = (jnp.arange(seq) % tk_p) // SLC_BLOCK
        onehot = (blk_in_chunk[:, None] == jnp.arange(128 - HEAD_DIM)[None, :]).astype(BF)
        ks = jnp.concatenate(
            [heads_major(rows[:, :, 2 * KV_W : 3 * KV_W]), jnp.broadcast_to(onehot, (bp, N_KV_HEADS) + onehot.shape)],
            axis=3,
        )
        ctx_p.update(
            rows=rows, win=win,
            kc=heads_major(cmp[:, :, :KV_W]), vc_t=values_t(cmp[:, :, KV_W:]),
            ks=ks, vs_t=values_t(rows[:, :, 3 * KV_W :]),
            kw=heads_major(win[:, :, :KV_W]), vw_t=values_t(win[:, :, KV_W:]),
        )

    def attend_prompt(x, mod, l, j):
        qn_t, qr_t, gates_t = _qg_proj(x, mod, norm_g, wq_b[j], wg_b[j], cos_p, sin_p, l, tm_p, True)
        o_t = _attn_prompt(
            qn_t, qr_t, gates_t, ctx_p["kc"], ctx_p["vc_t"], ctx_p["ks"],
            ctx_p["vs_t"], ctx_p["kw"], ctx_p["vw_t"], ov_t_p, ctab_p, wtab_p, tri_p, ns_p, Q_BLOCK, tk_p,
        )
        return _out_proj(o_t, x, mod, norm_g, w_o_b[j], l, tm_p, True)

    zero_row = jnp.zeros((bp, 1, d), F32)
    y_prompt, conv_p = trunk(x_prompt, mod_p, tm_p, lambda l: (zero_row, zero_row), True, build_prompt, attend_prompt)
    kv_rows_p = ctx_p["rows"].reshape(bp, seq, 4, N_KV_HEADS, HEAD_DIM)
    wbp = min(WINDOW, seq)
    win_p = ctx_p["win"][:, seq - wbp :].reshape(bp, wbp, 2, N_KV_HEADS, HEAD_DIM)
    conv_p = jnp.stack(conv_p)

    xs = x_sample.reshape(1, db, d)
    tm_s = db
    cos_s, sin_s = _rope_tables(jnp.full((db,), past_len, jnp.int32))
    total = past_len + 1
    nc_s = (total - CMP_LEN) // CMP_STRIDE + 1
    ns_s = -(-total // SLC_BLOCK)
    ng_s = past_len // CMP_STRIDE
    assert ns_s <= NS_PAD and ng_s <= 128 and past_len // SLC_BLOCK < ns_s
    cache2 = cache_kv.astype(BF).reshape(cache_kv.shape[0], page, 4 * KV_W)
    win_state = state_win_kv.reshape(db, win_buf, 2 * KV_W)
    ov_s = _overlap(ng_s, NS_PAD).astype(BF)
    expand = (jnp.arange(past_len)[None, :] // SLC_BLOCK == jnp.arange(NS_PAD)[:, None]).astype(BF)
    head_kv = jnp.arange(N_HEADS) // GROUP
    gsum = (head_kv[:, None] == head_kv[None, :]).astype(BF)
    bd_mask = (head_kv[:, None] == jnp.arange(N_KV_HEADS)[None, :]).astype(BF)
    ctx_s = {}

    def build_sample(x):
        rows, win = _kv_proj(x, kv_norm_g, w_kv_b, cos_s, sin_s, tm_s)
        ctx_s.update(
            rows=rows.reshape(db, 1, 4 * KV_W), win=win.reshape(db, 1, 2 * KV_W),
            cmp=_compress_sample(cache2, page_table, pe2, w1bd, w2bd, total, _pick_tile(db, 4)),
        )

    def block_diag(q):
        qh = q.reshape(db, N_HEADS, 1, HEAD_DIM) * bd_mask[None, :, :, None]
        return qh.reshape(db, N_HEADS, KV_W)

    def attend_sample(x, mod, l, j):
        qn, qr, gates = _qg_proj(x, mod, norm_g, wq_b[j], wg_b[j], cos_s, sin_s, l, tm_s, False)
        gates = gates[0, :, : N_KV_HEADS * GATE_ROWS].reshape(db, N_KV_HEADS, GATE_ROWS)[:, :, : 3 * GROUP]
        o_full = _attn_sample(
            cache2, page_table, block_diag(qn[0]), block_diag(qr[0]), gates.reshape(db, N_HEADS, 3),
            ctx_s["cmp"], ctx_s["rows"], win_state, ctx_s["win"], ov_s, expand, gsum, nc_s, ns_s,
        )
        o = jnp.take_along_axis(
            o_full.reshape(db, N_HEADS, N_KV_HEADS, HEAD_DIM), head_kv[None, :, None, None], axis=2
        ).reshape(1, db, nq)
        return _out_proj(o.astype(BF), x, mod, norm_g, w_o_b[j], l, tm_s, False)

    def conv_prev_s(l):
        return state_conv[l, :, 0, :].reshape(1, db, d), state_conv[l, :, 1, :].reshape(1, db, d)

    y_s, conv_u = trunk(xs, mod_s, tm_s, conv_prev_s, False, build_sample, attend_sample)
    y_sample = y_s.reshape(db, 1, d)
    kv_rows_s = ctx_s["rows"].reshape(db, 1, 4, N_KV_HEADS, HEAD_DIM)
    win_s = jnp.concatenate([state_win_kv, ctx_s["win"].reshape(db, 1, 2, N_KV_HEADS, HEAD_DIM)], axis=1)[:, 1:]
    conv_s = jnp.stack([jnp.stack([state_conv[l, :, 1, :], conv_u[l][0]], axis=1) for l in range(n_a)])

    return (y_prompt, y_sample, kv_rows_p, win_p, conv_p, kv_rows_s, win_s, conv_s)
```
